```python
import jax
import jax.numpy as jnp
from jax import lax
import numpy as np

D_MODEL = 1024
BATCH = 8
SEQ = 2048
DEPTH = 4
DEC_BATCH = 128
DEC_SEQ = 1
PAST_LEN = 16384
PAGE_SIZE = 128

N_META = 16
N_MIXERS = 2
N_RET = (DEPTH + 1) // 2
N_RWKV = DEPTH // 2
RET_HEADS = 4
RET_DK = D_MODEL // RET_HEADS
RET_DV = 2 * RET_DK
RET_VDIM = RET_HEADS * RET_DV
RET_CHUNK = 128
RWKV_HEAD = 64
RWKV_HEADS = D_MODEL // RWKV_HEAD
LORA_W = 64
LORA_A = 64
LORA_V = 32
LORA_G = 160
D_FF = 2816
CONV_W = 3
RMS_EPS = 1e-6
RET_GN_EPS = 1e-5
RWKV_GN_EPS = 64e-5

kernel_name = 'hybrid_retention_rwkv7_convffn_step'


def rmsnorm(x, g):
    return x * lax.rsqrt(jnp.mean(jnp.square(x), axis=-1, keepdims=True) + RMS_EPS) * g


def head_norm(y, eps):
    yc = y - jnp.mean(y, axis=-1, keepdims=True)
    return yc * lax.rsqrt(jnp.mean(jnp.square(yc), axis=-1, keepdims=True) + eps)


def rotary(x, pos):
    half = x.shape[-1] // 2
    inv = 1.0 / (10000.0 ** jnp.linspace(0.0, 1.0, half, dtype=jnp.float32))
    ang = pos.astype(jnp.float32)[:, None] * inv[None, :]
    cos = jnp.cos(ang)[None, :, None, :]
    sin = jnp.sin(ang)[None, :, None, :]
    x1, x2 = x[..., :half], x[..., half:]
    return jnp.concatenate([x1 * cos - x2 * sin, x1 * sin + x2 * cos], axis=-1)


def retention_chunk(q, k, v, S, log_g):
    L = q.shape[2]
    idx = jnp.arange(L, dtype=jnp.float32)
    rel = idx[:, None] - idx[None, :]
    dec = jnp.where(rel >= 0, jnp.exp(log_g[:, None, None] * jnp.maximum(rel, 0.0)), 0.0)
    scores = jnp.einsum('bhid,bhjd->bhij', q, k) * dec[None]
    o = (jnp.einsum('bhij,bhje->bhie', scores, v)
         + jnp.exp(log_g[:, None] * (idx + 1.0))[None, :, :, None] * jnp.einsum('bhid,bhde->bhie', q, S))
    kd = k * jnp.exp(log_g[:, None] * (L - 1.0 - idx))[None, :, :, None]
    S_new = jnp.exp(log_g * L)[None, :, None, None] * S + jnp.einsum('bhjd,bhje->bhde', kd, v)
    return o, S_new


def retention_seq(q, k, v, S, log_g, lead):
    B, H, L, _ = q.shape
    S = S.astype(jnp.float32)
    if lead:
        padw = ((0, 0), (0, 0), (lead, 0), (0, 0))
        q, k, v = jnp.pad(q, padw), jnp.pad(k, padw), jnp.pad(v, padw)
    total = lead + L
    n_full = total // RET_CHUNK
    rem = total - n_full * RET_CHUNK
    outs = []
    if n_full:
        def to_chunks(t):
            t = t[:, :, :n_full * RET_CHUNK].reshape(B, H, n_full, RET_CHUNK, t.shape[-1])
            return t.transpose(2, 0, 1, 3, 4)

        def body(S_c, qkv):
            qc, kc, vc = qkv
            o_c, S_c = retention_chunk(qc, kc, vc, S_c, log_g)
            return S_c, o_c

        S, o = lax.scan(body, S, (to_chunks(q), to_chunks(k), to_chunks(v)))
        outs.append(o.transpose(1, 2, 0, 3, 4).reshape(B, H, n_full * RET_CHUNK, RET_DV))
    if rem:
        s0 = n_full * RET_CHUNK
        o, S = retention_chunk(q[:, :, s0:], k[:, :, s0:], v[:, :, s0:], S, log_g)
        outs.append(o)
    o = jnp.concatenate(outs, axis=2)[:, :, lead:]
    return o, S


def retention_mixer(x, S, pos, lead, w_in, gn_w, w_out):
    B, T, _ = x.shape
    proj = x @ w_in
    qk = RET_HEADS * RET_DK
    q = rotary(proj[..., :qk].reshape(B, T, RET_HEADS, RET_DK), pos)
    k = rotary(proj[..., qk:2 * qk].reshape(B, T, RET_HEADS, RET_DK), pos) * (RET_DK ** -0.5)
    v = proj[..., 2 * qk:2 * qk + RET_VDIM].reshape(B, T, RET_HEADS, RET_DV)
    g = proj[..., 2 * qk + RET_VDIM:]
    log_g = jnp.log(1.0 - 2.0 ** (-5.0 - jnp.arange(RET_HEADS, dtype=jnp.float32)))
    o, S = retention_seq(q.transpose(0, 2, 1, 3), k.transpose(0, 2, 1, 3), v.transpose(0, 2, 1, 3), S, log_g, lead)
    o = head_norm(o.transpose(0, 2, 1, 3), RET_GN_EPS) * gn_w.reshape(RET_HEADS, RET_DV)
    o = jax.nn.silu(g) * o.reshape(B, T, RET_VDIM)
    return o @ w_out, S


def rwkv7_mixer(x, prev, S, v_first, vres, p, j):
    B, T, D = x.shape
    H, N = RWKV_HEADS, RWKV_HEAD
    xprev = jnp.concatenate([prev[:, None, :].astype(jnp.float32), x[:, :-1]], axis=1)
    xs = x[None] + (xprev - x)[None] * p['rwkv_mu'][j][:, None, None, :]
    xr, xw, xk, xv, xa, xg = xs[0], xs[1], xs[2], xs[3], xs[4], xs[5]
    rkv = jnp.einsum('nbtd,nde->nbte', jnp.stack([xr, xk, xv]), p['rwkv_w_rkv'][j])
    r, k, v = rkv[0], rkv[1], rkv[2]
    logw = -jax.nn.softplus(-(p['rwkv_w0'][j] + jnp.tanh(xw @ p['rwkv_w1'][j]) @ p['rwkv_w2'][j])) - 0.5
    w = jnp.exp(-jnp.exp(logw))
    a = jax.nn.sigmoid(p['rwkv_a0'][j] + (xa @ p['rwkv_a1'][j]) @ p['rwkv_a2'][j])
    g = jax.nn.sigmoid(xg @ p['rwkv_g1'][j]) @ p['rwkv_g2'][j]
    kk = (k * p['rwkv_k_k'][j]).reshape(B, T, H, N)
    kk = kk * lax.rsqrt(jnp.maximum(jnp.sum(jnp.square(kk), axis=-1, keepdims=True), 1e-12))
    k = k * (1.0 + (a - 1.0) * p['rwkv_k_a'][j])
    if vres is None:
        v_first = v
    else:
        v0, v1, v2 = vres
        v = v + (v_first - v) * jax.nn.sigmoid(v0 + (xv @ v1) @ v2)
    rh, wh, kh, vh, ah = [t.reshape(B, T, H, N) for t in (r, w, k, v, a)]

    def step(S_t, inp):
        r_t, w_t, k_t, v_t, kk_t, a_t = inp
        sa = jnp.einsum('bhij,bhj->bhi', S_t, -kk_t)
        S_t = (S_t * w_t[:, :, None, :] + sa[..., None] * (kk_t * a_t)[:, :, None, :]
               + v_t[..., None] * k_t[:, :, None, :])
        return S_t, jnp.einsum('bhij,bhj->bhi', S_t, r_t)

    seq = tuple(jnp.swapaxes(t, 0, 1) for t in (rh, wh, kh, vh, kk, ah))
    S, y = lax.scan(step, S.astype(jnp.float32), seq)
    y = jnp.swapaxes(y, 0, 1)
    y = head_norm(y, RWKV_GN_EPS).reshape(B, T, D) * p['rwkv_ln_w'][j] + p['rwkv_ln_b'][j]
    bonus = jnp.sum(rh * kh * p['rwkv_r_k'][j], axis=-1, keepdims=True) * vh
    out = ((y + bonus.reshape(B, T, D)) * g) @ p['rwkv_w_o'][j]
    return out, x[:, -1], S, v_first


def conv_ffn(x, buf, w_ug, conv_w, conv_b, w_d):
    T = x.shape[1]
    ug = x @ w_ug
    u, gate = ug[..., :D_FF], ug[..., D_FF:]
    cat = jnp.concatenate([buf.astype(jnp.float32), gate], axis=1)
    conv = conv_b + sum(cat[:, t:t + T] * conv_w[t] for t in range(CONV_W))
    return (jax.nn.silu(conv) * u) @ w_d, cat[:, T:]


def trunk(x, pos, lead, ret_S, wkv_S, shift, conv_buf, p):
    x = x.astype(jnp.float32)
    new_ret, new_wkv, new_shift, new_conv = [], [], [], []
    v_first = None
    for i in range(DEPTH):
        j = i // N_MIXERS
        h = rmsnorm(x, p['norm_mix'][i])
        if i % N_MIXERS == 0:
            o, s = retention_mixer(h, ret_S[j], pos, lead, p['ret_w_in'][j], p['ret_gn_w'][j], p['ret_w_out'][j])
            new_ret.append(s)
        else:
            vres = None if j == 0 else (p['rwkv_v0'][j - 1], p['rwkv_v1'][j - 1], p['rwkv_v2'][j - 1])
            o, sh, s, v_first = rwkv7_mixer(h, shift[j], wkv_S[j], v_first, vres, p, j)
            new_wkv.append(s)
            new_shift.append(sh)
        x = x + o
        f, cb = conv_ffn(rmsnorm(x, p['norm_ffn'][i]), conv_buf[i], p['ffn_w_ug'][i],
                         p['ffn_conv_w'][i], p['ffn_conv_b'][i], p['ffn_w_d'][i])
        new_conv.append(cb)
        x = x + f
    return (rmsnorm(x, p['norm_final']), jnp.stack(new_ret), jnp.stack(new_wkv),
            jnp.stack(new_shift), jnp.stack(new_conv))


def setup_inputs(seed: int = 0) -> dict:
    key = jax.random.key(seed)
    ks = iter(jax.random.split(key, 48))
    f32 = jnp.float32

    def nrm(shape, scale):
        return jax.random.normal(next(ks), shape, f32) * scale

    Dm = D_MODEL
    nv = N_RWKV - 1
    inp = {}
    inp['x_prompt'] = nrm((BATCH, SEQ, Dm), 1.0)
    inp['x_sample'] = nrm((DEC_BATCH, DEC_SEQ, Dm), 1.0)
    inp['state_ret'] = nrm((N_RET, DEC_BATCH, RET_HEADS, RET_DK, RET_DV), 0.5)
    inp['state_wkv'] = nrm((N_RWKV, DEC_BATCH, RWKV_HEADS, RWKV_HEAD, RWKV_HEAD), 0.5)
    inp['state_shift'] = nrm((N_RWKV, DEC_BATCH, Dm), 1.0)
    inp['state_conv'] = nrm((DEPTH, DEC_BATCH, CONV_W - 1, D_FF), 1.0)
    inp['meta_tokens'] = nrm((N_META, Dm), 1.0)
    inp['norm_mix'] = 1.0 + nrm((DEPTH, Dm), 0.01)
    inp['norm_ffn'] = 1.0 + nrm((DEPTH, Dm), 0.01)
    inp['norm_final'] = 1.0 + nrm((Dm,), 0.01)
    inp['ret_w_in'] = nrm((N_RET, Dm, 2 * RET_HEADS * RET_DK + 2 * RET_VDIM), Dm ** -0.5)
    inp['ret_gn_w'] = 1.0 + nrm((N_RET, RET_VDIM), 0.01)
    inp['ret_w_out'] = nrm((N_RET, RET_VDIM, Dm), RET_VDIM ** -0.5)
    inp['rwkv_mu'] = jax.random.uniform(next(ks), (N_RWKV, 6, Dm), f32)
    inp['rwkv_w_rkv'] = nrm((N_RWKV, 3, Dm, Dm), Dm ** -0.5)
    inp['rwkv_w0'] = nrm((N_RWKV, Dm), 0.5)
    inp['rwkv_w1'] = nrm((N_RWKV, Dm, LORA_W), Dm ** -0.5)
    inp['rwkv_w2'] = nrm((N_RWKV, LORA_W, Dm), 0.5 * LORA_W ** -0.5)
    inp['rwkv_a0'] = nrm((N_RWKV, Dm), 0.5)
    inp['rwkv_a1'] = nrm((N_RWKV, Dm, LORA_A), Dm ** -0.5)
    inp['rwkv_a2'] = nrm((N_RWKV, LORA_A, Dm), 0.5 * LORA_A ** -0.5)
    inp['rwkv_v0'] = nrm((nv, Dm), 0.5)
    inp['rwkv_v1'] = nrm((nv, Dm, LORA_V), Dm ** -0.5)
    inp['rwkv_v2'] = nrm((nv, LORA_V, Dm), 0.5 * LORA_V ** -0.5)
    inp['rwkv_g1'] = nrm((N_RWKV, Dm, LORA_G), Dm ** -0.5)
    inp['rwkv_g2'] = nrm((N_RWKV, LORA_G, Dm), LORA_G ** -0.5)
    inp['rwkv_k_k'] = 1.0 + nrm((N_RWKV, Dm), 0.1)
    inp['rwkv_k_a'] = 1.0 + nrm((N_RWKV, Dm), 0.1)
    inp['rwkv_r_k'] = nrm((N_RWKV, RWKV_HEADS, RWKV_HEAD), 0.1)
    inp['rwkv_ln_w'] = 1.0 + nrm((N_RWKV, Dm), 0.01)
    inp['rwkv_ln_b'] = nrm((N_RWKV, Dm), 0.01)
    inp['rwkv_w_o'] = nrm((N_RWKV, Dm, Dm), Dm ** -0.5)
    inp['ffn_w_ug'] = nrm((DEPTH, Dm, 2 * D_FF), Dm ** -0.5)
    inp['ffn_conv_w'] = nrm((DEPTH, CONV_W, D_FF), CONV_W ** -0.5)
    inp['ffn_conv_b'] = nrm((DEPTH, D_FF), 0.01)
    inp['ffn_w_d'] = nrm((DEPTH, D_FF, Dm), D_FF ** -0.5)
    return inp


def reference(x_prompt, x_sample, state_ret, state_wkv, state_shift, state_conv, meta_tokens,
              norm_mix, norm_ffn, norm_final, ret_w_in, ret_gn_w, ret_w_out,
              rwkv_mu, rwkv_w_rkv, rwkv_w0, rwkv_w1, rwkv_w2, rwkv_a0, rwkv_a1, rwkv_a2,
              rwkv_v0, rwkv_v1, rwkv_v2, rwkv_g1, rwkv_g2, rwkv_k_k, rwkv_k_a, rwkv_r_k,
              rwkv_ln_w, rwkv_ln_b, rwkv_w_o, ffn_w_ug, ffn_conv_w, ffn_conv_b, ffn_w_d):
    p = dict(norm_mix=norm_mix, norm_ffn=norm_ffn, norm_final=norm_final,
             ret_w_in=ret_w_in, ret_gn_w=ret_gn_w, ret_w_out=ret_w_out,
             rwkv_mu=rwkv_mu, rwkv_w_rkv=rwkv_w_rkv, rwkv_w0=rwkv_w0, rwkv_w1=rwkv_w1, rwkv_w2=rwkv_w2,
             rwkv_a0=rwkv_a0, rwkv_a1=rwkv_a1, rwkv_a2=rwkv_a2, rwkv_v0=rwkv_v0, rwkv_v1=rwkv_v1,
             rwkv_v2=rwkv_v2, rwkv_g1=rwkv_g1, rwkv_g2=rwkv_g2, rwkv_k_k=rwkv_k_k, rwkv_k_a=rwkv_k_a,
             rwkv_r_k=rwkv_r_k, rwkv_ln_w=rwkv_ln_w, rwkv_ln_b=rwkv_ln_b, rwkv_w_o=rwkv_w_o,
             ffn_w_ug=ffn_w_ug, ffn_conv_w=ffn_conv_w, ffn_conv_b=ffn_conv_b, ffn_w_d=ffn_w_d)
    f32 = jnp.float32
    B = x_prompt.shape[0]
    meta = jnp.broadcast_to(meta_tokens[None].astype(x_prompt.dtype), (B, N_META, D_MODEL))
    xp = jnp.concatenate([meta, x_prompt], axis=1)
    pos_p = jnp.arange(xp.shape[1])
    lead = (-N_META) % RET_CHUNK
    yp, p_ret, p_wkv, p_shift, p_conv = trunk(
        xp, pos_p, lead,
        jnp.zeros((N_RET, B, RET_HEADS, RET_DK, RET_DV), f32),
        jnp.zeros((N_RWKV, B, RWKV_HEADS, RWKV_HEAD, RWKV_HEAD), f32),
        jnp.zeros((N_RWKV, B, D_MODEL), f32),
        jnp.zeros((DEPTH, B, CONV_W - 1, D_FF), f32), p)
    y_prompt = yp[:, N_META:].astype(x_prompt.dtype)
    pos_s = PAST_LEN + jnp.arange(x_sample.shape[1])
    ys, s_ret, s_wkv, s_shift, s_conv = trunk(
        x_sample, pos_s, 0, state_ret, state_wkv, state_shift, state_conv, p)
    y_sample = ys.astype(x_sample.dtype)
    return (y_prompt, y_sample, p_ret, p_wkv, p_shift, p_conv, s_ret, s_wkv, s_shift, s_conv)
```

```python
import functools
import math

import jax
import jax.numpy as jnp
from jax import lax
from jax.experimental import pallas as pl
from jax.experimental.pallas import tpu as pltpu

F32 = jnp.float32
BF16 = jnp.bfloat16

D_MODEL = 1024
N_META = 16
PAST_LEN = 16384
RET_HEADS = 4
RET_DK = D_MODEL // RET_HEADS
RET_DV = 2 * RET_DK
RET_VDIM = RET_HEADS * RET_DV
RET_CHUNK = 128
RWKV_HEAD = 64
RWKV_HEADS = D_MODEL // RWKV_HEAD
D_FF = 2816
RMS_EPS = 1e-6
RET_GN_EPS = 1e-5
RWKV_GN_EPS = 64e-5

LANES = 128
WKV_CHUNK = 64
VMEM_LIMIT = 56 * 1024 * 1024
HI = lax.Precision.HIGHEST

NT_DIMS = (((1,), (1,)), ((), ()))
TN_DIMS = (((0,), (0,)), ((), ()))


def _params(*sem):
    return pltpu.CompilerParams(dimension_semantics=sem, vmem_limit_bytes=VMEM_LIMIT)


def _rms(x, g):
    return x * lax.rsqrt(jnp.mean(x * x, axis=-1, keepdims=True) + RMS_EPS) * g


def _sigmoid(x):
    return 1.0 / (1.0 + jnp.exp(-x))


def _row_tile(rows_per_seq, cap):
    best = None
    for t in range(16, min(rows_per_seq, cap) + 1, 16):
        if rows_per_seq % t == 0:
            best = t
    assert best is not None, rows_per_seq
    return best


def _rot_table_kernel(inv_ref, cos_ref, sin_ref, *, pos0, step):
    rows = lax.broadcasted_iota(jnp.int32, cos_ref.shape, 0)
    pos = (rows * step + pos0).astype(F32)
    ang = pos * inv_ref[...]
    cos_ref[...] = jnp.cos(ang)
    sin_ref[...] = jnp.sin(ang)


def rot_table(n_rows, pos0, step):
    half = RET_DK // 2
    inv = (1.0 / (10000.0 ** jnp.linspace(0.0, 1.0, half, dtype=F32))).reshape(1, half)
    return pl.pallas_call(
        functools.partial(_rot_table_kernel, pos0=pos0, step=step),
        out_shape=(jax.ShapeDtypeStruct((n_rows, half), F32),) * 2,
        name="rot_table",
    )(inv)


def _norm_proj_kernel(x_ref, g_ref, w_ref, o_ref, h_ref):
    @pl.when(pl.program_id(1) == 0)
    def _():
        h_ref[...] = _rms(x_ref[...], g_ref[...]).astype(BF16)

    o_ref[...] = jnp.dot(h_ref[...], w_ref[...], preferred_element_type=F32).astype(o_ref.dtype)


def norm_proj(x, g, w, tm, tn, out_dtype):
    m, d = x.shape
    n = w.shape[1]
    return pl.pallas_call(
        _norm_proj_kernel,
        grid=(m // tm, n // tn),
        in_specs=[pl.BlockSpec((tm, d), lambda i, j: (i, 0)),
                  pl.BlockSpec((1, d), lambda i, j: (0, 0)),
                  pl.BlockSpec((d, tn), lambda i, j: (0, j))],
        out_specs=pl.BlockSpec((tm, tn), lambda i, j: (i, j)),
        out_shape=jax.ShapeDtypeStruct((m, n), out_dtype),
        scratch_shapes=[pltpu.VMEM((tm, d), BF16)],
        compiler_params=_params("parallel", "arbitrary"),
        name="norm_proj",
    )(x, g, w)


def _matmul_res_kernel(a_ref, w_ref, x_ref, o_ref, *, tiles_per_seq, lead):
    y = x_ref[...] + jnp.dot(a_ref[...], w_ref[...], preferred_element_type=F32)
    if lead:
        tm = y.shape[0]
        row = (pl.program_id(0) % tiles_per_seq) * tm + lax.broadcasted_iota(jnp.int32, (tm, 1), 0)
        y = jnp.where(row >= lead, y, 0.0)
    o_ref[...] = y


def matmul_res(a, w, x, tm, tiles_per_seq, lead):
    m, k = a.shape
    n = w.shape[1]
    return pl.pallas_call(
        functools.partial(_matmul_res_kernel, tiles_per_seq=tiles_per_seq, lead=lead),
        grid=(m // tm,),
        in_specs=[pl.BlockSpec((tm, k), lambda i: (i, 0)),
                  pl.BlockSpec((k, n), lambda i: (0, 0)),
                  pl.BlockSpec((tm, n), lambda i: (i, 0))],
        out_specs=pl.BlockSpec((tm, n), lambda i: (i, 0)),
        out_shape=jax.ShapeDtypeStruct((m, n), F32),
        compiler_params=_params("parallel"),
        name="matmul_res",
    )(a, w, x)


def _final_norm_kernel(x_ref, g_ref, o_ref):
    o_ref[...] = _rms(x_ref[...], g_ref[...])


def final_norm(x, g, n_seq, chunks_per_seq, skip_chunks, rows):
    d = x.shape[1]
    keep = chunks_per_seq - skip_chunks
    return pl.pallas_call(
        _final_norm_kernel,
        grid=(n_seq, keep),
        in_specs=[pl.BlockSpec((rows, d), lambda b, c: (b * chunks_per_seq + skip_chunks + c, 0)),
                  pl.BlockSpec((1, d), lambda b, c: (0, 0))],
        out_specs=pl.BlockSpec((rows, d), lambda b, c: (b * keep + c, 0)),
        out_shape=jax.ShapeDtypeStruct((n_seq * keep * rows, d), F32),
        compiler_params=_params("parallel", "parallel"),
        name="final_norm",
    )(x, g)


def _rotate(x, cos, sin):
    half = RET_DK // 2
    x1, x2 = x[:, :half], x[:, half:]
    return jnp.concatenate([x1 * cos - x2 * sin, x1 * sin + x2 * cos], axis=1)


def _gated_head_norm(o, g, gnw):
    oc = o - jnp.mean(o, axis=-1, keepdims=True)
    on = oc * lax.rsqrt(jnp.mean(oc * oc, axis=-1, keepdims=True) + RET_GN_EPS)
    return (g * _sigmoid(g)) * (on * gnw)


def _retention_kernel(q_ref, k_ref, v_ref, g_ref, cos_ref, sin_ref, gnw_ref, lg_ref,
                      o_ref, s_out_ref, s_ref):
    c = pl.program_id(2)
    L = RET_CHUNK

    @pl.when(c == 0)
    def _():
        s_ref[...] = jnp.zeros_like(s_ref)

    lg = lg_ref[0:1, 0:1]
    cos, sin = cos_ref[...], sin_ref[...]
    q = _rotate(q_ref[...].astype(F32), cos, sin)
    k = _rotate(k_ref[...].astype(F32), cos, sin) * (RET_DK ** -0.5)
    v = v_ref[...]
    ri = lax.broadcasted_iota(jnp.int32, (L, 1), 0).astype(F32)
    ci = lax.broadcasted_iota(jnp.int32, (1, L), 1).astype(F32)
    rel = ri - ci
    dec = jnp.where(rel >= 0, jnp.exp(lg * jnp.maximum(rel, 0.0)), 0.0)
    qb = q.astype(BF16)
    s_old = s_ref[...]
    scores = lax.dot_general(qb, k.astype(BF16), NT_DIMS, preferred_element_type=F32) * dec
    o = (jnp.dot(scores.astype(BF16), v, preferred_element_type=F32)
         + jnp.exp(lg * (ri + 1.0)) * jnp.dot(qb, s_old.astype(BF16), preferred_element_type=F32))
    kd = (k * jnp.exp(lg * (L - 1.0 - ri))).astype(BF16)
    s_new = jnp.exp(lg * L) * s_old + lax.dot_general(kd, v, TN_DIMS, preferred_element_type=F32)
    s_ref[...] = s_new
    o_ref[...] = _gated_head_norm(o, g_ref[...].astype(F32), gnw_ref[...]).astype(o_ref.dtype)

    @pl.when(c == pl.num_programs(2) - 1)
    def _():
        s_out_ref[0, 0] = s_new


def _log_g(h):
    return math.log(1.0 - 2.0 ** (-5.0 - h))


def retention_prompt(proj, cos, sin, gnw, n_seq, chunks):
    L = RET_CHUNK
    m = proj.shape[0]
    qk_blocks = RET_HEADS
    v_blocks = 2 * RET_HEADS * RET_DK // RET_DV
    lg = jnp.repeat(jnp.asarray([_log_g(h) for h in range(RET_HEADS)], F32), 8)[:, None]
    lg = jnp.broadcast_to(lg, (8 * RET_HEADS, LANES))
    row = lambda b, h, c: b * chunks + c
    return pl.pallas_call(
        _retention_kernel,
        grid=(n_seq, RET_HEADS, chunks),
        in_specs=[pl.BlockSpec((L, RET_DK), lambda b, h, c: (row(b, h, c), h)),
                  pl.BlockSpec((L, RET_DK), lambda b, h, c: (row(b, h, c), qk_blocks + h)),
                  pl.BlockSpec((L, RET_DV), lambda b, h, c: (row(b, h, c), v_blocks + h)),
                  pl.BlockSpec((L, RET_DV), lambda b, h, c: (row(b, h, c), v_blocks + RET_HEADS + h)),
                  pl.BlockSpec((L, RET_DK // 2), lambda b, h, c: (c, 0)),
                  pl.BlockSpec((L, RET_DK // 2), lambda b, h, c: (c, 0)),
                  pl.BlockSpec((1, RET_DV), lambda b, h, c: (0, h)),
                  pl.BlockSpec((8, LANES), lambda b, h, c: (h, 0))],
        out_specs=[pl.BlockSpec((L, RET_DV), lambda b, h, c: (row(b, h, c), h)),
                   pl.BlockSpec((1, 1, RET_DK, RET_DV), lambda b, h, c: (b, h, 0, 0))],
        out_shape=[jax.ShapeDtypeStruct((m, RET_VDIM), BF16),
                   jax.ShapeDtypeStruct((n_seq, RET_HEADS, RET_DK, RET_DV), F32)],
        scratch_shapes=[pltpu.VMEM((RET_DK, RET_DV), F32)],
        compiler_params=_params("parallel", "parallel", "arbitrary"),
        name="retention_prompt",
    )(proj, proj, proj, proj, cos, sin, gnw, lg)


def _to_column(row, n):
    eye = lax.broadcasted_iota(jnp.int32, (n, n), 0) == lax.broadcasted_iota(jnp.int32, (n, n), 1)
    return jnp.sum(jnp.where(eye, jnp.broadcast_to(row, (n, n)), 0.0), axis=-1, keepdims=True)


def _retention_step_kernel(p_ref, cos_ref, sin_ref, gnw_ref, s_ref, o_ref, s_out_ref):
    cos, sin = cos_ref[0:1, :], sin_ref[0:1, :]
    qk = RET_HEADS * RET_DK
    for h in range(RET_HEADS):
        decay = math.exp(_log_g(h))
        q = _rotate(p_ref[0, :, h * RET_DK:(h + 1) * RET_DK], cos, sin)
        k = _rotate(p_ref[0, :, qk + h * RET_DK:qk + (h + 1) * RET_DK], cos, sin) * (RET_DK ** -0.5)
        v = p_ref[0, :, 2 * qk + h * RET_DV:2 * qk + (h + 1) * RET_DV]
        g = p_ref[0, :, 2 * qk + RET_VDIM + h * RET_DV:2 * qk + RET_VDIM + (h + 1) * RET_DV]
        s_old = s_ref[0, h]
        qs = jnp.sum(s_old * _to_column(q, RET_DK), axis=0, keepdims=True)
        o = jnp.sum(q * k, axis=-1, keepdims=True) * v + decay * qs
        s_out_ref[0, h] = decay * s_old + _to_column(k, RET_DK) * v
        o_ref[0, :, h * RET_DV:(h + 1) * RET_DV] = _gated_head_norm(
            o, g, gnw_ref[:, h * RET_DV:(h + 1) * RET_DV]).astype(o_ref.dtype)


def retention_step(proj, cos, sin, gnw, state):
    n = proj.shape[0]
    return pl.pallas_call(
        _retention_step_kernel,
        grid=(n,),
        in_specs=[pl.BlockSpec((1, 1, proj.shape[2]), lambda b: (b, 0, 0)),
                  pl.BlockSpec(cos.shape, lambda b: (0, 0)),
                  pl.BlockSpec(sin.shape, lambda b: (0, 0)),
                  pl.BlockSpec((1, RET_VDIM), lambda b: (0, 0)),
                  pl.BlockSpec((1, RET_HEADS, RET_DK, RET_DV), lambda b: (b, 0, 0, 0))],
        out_specs=[pl.BlockSpec((1, 1, RET_VDIM), lambda b: (b, 0, 0)),
                   pl.BlockSpec((1, RET_HEADS, RET_DK, RET_DV), lambda b: (b, 0, 0, 0))],
        out_shape=[jax.ShapeDtypeStruct((n, 1, RET_VDIM), BF16),
                   jax.ShapeDtypeStruct(state.shape, F32)],
        compiler_params=_params("parallel"),
        name="retention_step",
    )(proj, cos, sin, gnw, state)


def _conv_gate(gate, p1, p2, cw_ref, cb_ref):
    rows = lax.broadcasted_iota(jnp.int32, (gate.shape[0], 1), 0)
    g1 = jnp.where(rows == 0, p1, pltpu.roll(gate, 1, axis=0))
    g2 = jnp.where(rows == 0, p2, jnp.where(rows == 1, p1, pltpu.roll(gate, 2, axis=0)))
    return cb_ref[...] + g2 * cw_ref[0:1, :] + g1 * cw_ref[1:2, :] + gate * cw_ref[2:3, :]


def _ffn_up_seq_kernel(x_ref, g_ref, wu_ref, wg_ref, cw_ref, cb_ref, a_ref, nc_ref, h_ref, carry_ref,
                       *, tiles_per_seq):
    m, n = pl.program_id(0), pl.program_id(1)

    @pl.when(n == 0)
    def _():
        h_ref[...] = _rms(x_ref[...], g_ref[...]).astype(BF16)

    h = h_ref[...]
    u = jnp.dot(h, wu_ref[...], preferred_element_type=F32)
    gate = jnp.dot(h, wg_ref[...], preferred_element_type=F32)
    tm = gate.shape[0]

    @pl.when(m % tiles_per_seq == 0)
    def _():
        carry_ref[n] = jnp.zeros(carry_ref.shape[1:], F32)

    prev = carry_ref[n]
    conv = _conv_gate(gate, prev[1:2, :], prev[0:1, :], cw_ref, cb_ref)
    a_ref[...] = (conv * _sigmoid(conv) * u).astype(a_ref.dtype)
    last = gate[tm - 2:tm, :]
    carry_ref[n, 0:2, :] = last
    tn = gate.shape[1]
    for j in range(nc_ref.shape[2] // tn):
        @pl.when(n == j)
        def _():
            nc_ref[0, :, j * tn:(j + 1) * tn] = last


def ffn_up_seq(x, g, w_ug, cw, cb, tm, tn, n_seq, tiles_per_seq):
    m, d = x.shape
    nt = D_FF // tn
    return pl.pallas_call(
        functools.partial(_ffn_up_seq_kernel, tiles_per_seq=tiles_per_seq),
        grid=(m // tm, nt),
        in_specs=[pl.BlockSpec((tm, d), lambda i, j: (i, 0)),
                  pl.BlockSpec((1, d), lambda i, j: (0, 0)),
                  pl.BlockSpec((d, tn), lambda i, j: (0, j)),
                  pl.BlockSpec((d, tn), lambda i, j: (0, nt + j)),
                  pl.BlockSpec((3, tn), lambda i, j: (0, j)),
                  pl.BlockSpec((1, tn), lambda i, j: (0, j))],
        out_specs=[pl.BlockSpec((tm, tn), lambda i, j: (i, j)),
                   pl.BlockSpec((1, 2, D_FF), lambda i, j: (i // tiles_per_seq, 0, 0))],
        out_shape=[jax.ShapeDtypeStruct((m, D_FF), BF16),
                   jax.ShapeDtypeStruct((n_seq, 2, D_FF), F32)],
        scratch_shapes=[pltpu.VMEM((tm, d), BF16), pltpu.VMEM((nt, 8, tn), F32)],
        compiler_params=_params("arbitrary", "arbitrary"),
        name="ffn_up_seq",
    )(x, g, w_ug, w_ug, cw, cb)


def _ffn_up_step_kernel(x_ref, g_ref, wu_ref, wg_ref, cw_ref, cb_ref, b0_ref, b1_ref,
                        a_ref, n0_ref, n1_ref, h_ref):
    @pl.when(pl.program_id(0) == 0)
    def _():
        h_ref[...] = _rms(x_ref[...], g_ref[...]).astype(BF16)

    h = h_ref[...]
    u = jnp.dot(h, wu_ref[...], preferred_element_type=F32)
    gate = jnp.dot(h, wg_ref[...], preferred_element_type=F32)
    b1 = b1_ref[...]
    conv = cb_ref[...] + b0_ref[...] * cw_ref[0:1, :] + b1 * cw_ref[1:2, :] + gate * cw_ref[2:3, :]
    a_ref[...] = (conv * _sigmoid(conv) * u).astype(a_ref.dtype)
    n0_ref[...] = b1
    n1_ref[...] = gate


def ffn_up_step(x, g, w_ug, cw, cb, buf, tn):
    n_req, d = x.shape
    nt = D_FF // tn
    col = pl.BlockSpec((n_req, tn), lambda j: (0, j))
    return pl.pallas_call(
        _ffn_up_step_kernel,
        grid=(nt,),
        in_specs=[pl.BlockSpec((n_req, d), lambda j: (0, 0)),
                  pl.BlockSpec((1, d), lambda j: (0, 0)),
                  pl.BlockSpec((d, tn), lambda j: (0, j)),
                  pl.BlockSpec((d, tn), lambda j: (0, nt + j)),
                  pl.BlockSpec((3, tn), lambda j: (0, j)),
                  pl.BlockSpec((1, tn), lambda j: (0, j)),
                  col,
                  pl.BlockSpec((n_req, tn), lambda j: (0, nt + j))],
        out_specs=[col, col, col],
        out_shape=[jax.ShapeDtypeStruct((n_req, D_FF), BF16),
                   jax.ShapeDtypeStruct((n_req, D_FF), F32),
                   jax.ShapeDtypeStruct((n_req, D_FF), F32)],
        scratch_shapes=[pltpu.VMEM((n_req, d), BF16)],
        compiler_params=_params("arbitrary"),
        name="ffn_up_step",
    )(x, g, w_ug, w_ug, cw, cb, buf, buf)


def _pair_ones():
    r = lax.broadcasted_iota(jnp.int32, (LANES, LANES), 0) // RWKV_HEAD
    c = lax.broadcasted_iota(jnp.int32, (LANES, LANES), 1) // RWKV_HEAD
    return jnp.where(r == c, 1.0, 0.0).astype(BF16)


def _head_sum(x, ones):
    hi = x.astype(BF16)
    mid = (x - hi.astype(F32)).astype(BF16)
    lo = (x - hi.astype(F32) - mid.astype(F32)).astype(BF16)
    return (jnp.dot(hi, ones, preferred_element_type=F32) + jnp.dot(mid, ones, preferred_element_type=F32)
            + jnp.dot(lo, ones, preferred_element_type=F32))


def _rwkv_proj_body(h, hprev, mu_ref, wrkv_ref, w1_ref, w2_ref, a1_ref, a2_ref, g1_ref, g2_ref, vec_ref,
                    vres, outs):
    r_ref, k_ref, v_ref, kk_ref, ka_ref, ld_ref, g_ref = outs
    dx = hprev - h
    mix = lambda i: (h + dx * mu_ref[i:i + 1, :]).astype(BF16)
    xr, xw, xk, xv, xa, xg = (mix(i) for i in range(6))
    dot = lambda a, b: jnp.dot(a, b, preferred_element_type=F32)
    w0, a0, k_k, k_a = (vec_ref[i:i + 1, :] for i in range(4))
    r = dot(xr, wrkv_ref[0])
    k = dot(xk, wrkv_ref[1])
    v = dot(xv, wrkv_ref[2])
    z = w0 + dot(jnp.tanh(dot(xw, w1_ref[...])).astype(BF16), w2_ref[...])
    logw = -(jnp.maximum(-z, 0.0) + jnp.log(1.0 + jnp.exp(-jnp.abs(z)))) - 0.5
    ld_ref[...] = -jnp.exp(logw)
    a = _sigmoid(a0 + dot(dot(xa, a1_ref[...]).astype(BF16), a2_ref[...]))
    g_ref[...] = dot(_sigmoid(dot(xg, g1_ref[...])).astype(BF16), g2_ref[...])
    if vres is not None:
        vf_ref, v1_ref, v2_ref = vres
        v0 = vec_ref[4:5, :]
        v = v + (vf_ref[...] - v) * _sigmoid(v0 + dot(dot(xv, v1_ref[...]).astype(BF16), v2_ref[...]))
    kkr = k * k_k
    ones = _pair_ones()
    kk = jnp.concatenate(
        [kkr[:, t:t + LANES] * lax.rsqrt(jnp.maximum(
            _head_sum(kkr[:, t:t + LANES] * kkr[:, t:t + LANES], ones), 1e-12))
         for t in range(0, D_MODEL, LANES)], axis=1)
    r_ref[...] = r
    k_ref[...] = k * (1.0 + (a - 1.0) * k_a)
    v_ref[...] = v
    kk_ref[...] = kk
    ka_ref[...] = kk * a


def _rwkv_proj_seq_kernel(*refs, tiles_per_seq, has_vres):
    x_ref, gn_ref = refs[0], refs[1]
    weights = refs[2:11]
    n_in = 11 + (3 if has_vres else 0)
    vres = refs[11:14] if has_vres else None
    outs = refs[n_in:n_in + 7]
    shift_ref, carry_ref = refs[n_in + 7], refs[n_in + 8]
    m = pl.program_id(0)

    @pl.when(m % tiles_per_seq == 0)
    def _():
        carry_ref[...] = jnp.zeros_like(carry_ref)

    h = _rms(x_ref[...], gn_ref[...])
    tm = h.shape[0]
    rows = lax.broadcasted_iota(jnp.int32, (tm, 1), 0)
    hprev = jnp.where(rows == 0, carry_ref[0:1, :], pltpu.roll(h, 1, axis=0))
    _rwkv_proj_body(h, hprev, *weights, vres, outs)
    carry_ref[0:1, :] = h[tm - 1:tm, :]
    shift_ref[0] = h[tm - 1:tm, :]


def _rwkv_proj_step_kernel(*refs, has_vres):
    x_ref, gn_ref, prev_ref = refs[0], refs[1], refs[2]
    weights = refs[3:12]
    n_in = 12 + (3 if has_vres else 0)
    vres = refs[12:15] if has_vres else None
    outs = refs[n_in:n_in + 7]
    shift_ref = refs[n_in + 7]
    h = _rms(x_ref[...], gn_ref[...])
    _rwkv_proj_body(h, prev_ref[...], *weights, vres, outs)
    shift_ref[...] = h


def rwkv_proj(x, gn, prev, wts, vres, tm, n_seq, tiles_per_seq):
    m, d = x.shape
    full = lambda a: pl.BlockSpec(a.shape, lambda i, _n=a.ndim: (0,) * _n)
    rowblk = pl.BlockSpec((tm, d), lambda i: (i, 0))
    seq = prev is None
    ins = [x, gn] + ([] if seq else [prev]) + list(wts)
    specs = [rowblk, full(gn)] + ([] if seq else [rowblk]) + [full(a) for a in wts]
    if vres is not None:
        vf, v1, v2 = vres
        ins += [vf, v1, v2]
        specs += [rowblk, full(v1), full(v2)]
    out_shape = [jax.ShapeDtypeStruct((m, d), F32)] * 7
    out_specs = [rowblk] * 7
    if seq:
        out_shape.append(jax.ShapeDtypeStruct((n_seq, 1, d), F32))
        out_specs.append(pl.BlockSpec((1, 1, d), lambda i: (i // tiles_per_seq, 0, 0)))
        body = functools.partial(_rwkv_proj_seq_kernel, tiles_per_seq=tiles_per_seq, has_vres=vres is not None)
        scratch = [pltpu.VMEM((8, d), F32)]
    else:
        out_shape.append(jax.ShapeDtypeStruct((m, d), F32))
        out_specs.append(rowblk)
        body = functools.partial(_rwkv_proj_step_kernel, has_vres=vres is not None)
        scratch = []
    return pl.pallas_call(
        body,
        grid=(m // tm,),
        in_specs=specs,
        out_specs=out_specs,
        out_shape=out_shape,
        scratch_shapes=scratch,
        compiler_params=_params("arbitrary"),
        name="rwkv_proj_seq" if seq else "rwkv_proj_step",
    )(*ins)


def _wkv_epilogue(y, r, k, v, g, lnw, lnb, rk, ones):
    inv_n = 1.0 / RWKV_HEAD
    yc = y - _head_sum(y, ones) * inv_n
    yn = yc * lax.rsqrt(_head_sum(yc * yc, ones) * inv_n + RWKV_GN_EPS)
    bonus = _head_sum(r * k * rk, ones) * v
    return (yn * lnw + lnb + bonus) * g


def _stack_heads(x):
    first = lax.broadcasted_iota(jnp.int32, (1, LANES), 1) < RWKV_HEAD
    return jnp.concatenate([jnp.where(first, x, 0.0), jnp.where(first, 0.0, x)], axis=0)


def _wkv_seq_kernel(r_ref, k_ref, v_ref, kk_ref, ka_ref, ld_ref, g_ref, lnw_ref, lnb_ref, rk_ref,
                    z_ref, s_out_ref, s_ref, *, pairs):
    c = pl.program_id(2)
    L = WKV_CHUNK
    R = 2 * L

    @pl.when(c == 0)
    def _():
        s_ref[...] = jnp.zeros_like(s_ref)

    dot = lambda a, b: jnp.dot(a, b, preferred_element_type=F32, precision=HI)
    dot_nt = lambda a, b: lax.dot_general(a, b, NT_DIMS, preferred_element_type=F32, precision=HI)
    dot_tn = lambda a, b: lax.dot_general(a, b, TN_DIMS, preferred_element_type=F32, precision=HI)
    ones = _pair_ones()
    ti = lax.broadcasted_iota(jnp.int32, (L, L), 0)
    tj = lax.broadcasted_iota(jnp.int32, (L, L), 1)
    tri = jnp.where(ti >= tj, 1.0, 0.0)
    ri = lax.broadcasted_iota(jnp.int32, (R, R), 0)
    rj = lax.broadcasted_iota(jnp.int32, (R, R), 1)
    same = (ri // L) == (rj // L)
    lower = same & ((ri % L) > (rj % L))
    lower_eq = same & ((ri % L) >= (rj % L))
    eye = jnp.where(ri == rj, 1.0, 0.0)

    for p in range(pairs):
        sl = slice(p * LANES, (p + 1) * LANES)
        r, k, v, kk, ka, ld = (ref[:, sl] for ref in (r_ref, k_ref, v_ref, kk_ref, ka_ref, ld_ref))
        s_old = s_ref[p]
        cum = dot(tri, ld)
        tot = cum[L - 1:L, :]
        dec_in = jnp.exp(cum)
        dec_out = jnp.exp(-cum)
        dec_end = jnp.exp(tot - cum)
        a_s = _stack_heads(jnp.exp(cum - ld) * (-kk))
        r_s = _stack_heads(dec_in * r)
        b_s = _stack_heads(dec_out * ka)
        k_s = _stack_heads(dec_out * k)
        v_s = _stack_heads(v)
        n_ab = jnp.where(lower, dot_nt(a_s, b_s), 0.0)
        a_ak = jnp.where(lower, dot_nt(a_s, k_s), 0.0)
        a_rb = jnp.where(lower_eq, dot_nt(r_s, b_s), 0.0)
        a_rk = jnp.where(lower_eq, dot_nt(r_s, k_s), 0.0)
        inv = eye + n_ab
        pw = n_ab
        for _ in range(int(math.log2(L)) - 1):
            pw = dot(pw, pw)
            inv = inv + dot(inv, pw)
        u_s = dot(inv, dot_nt(a_s, s_old) + dot(a_ak, v_s))
        y_s = dot_nt(r_s, s_old) + dot(a_rb, u_s) + dot(a_rk, v_s)
        y = y_s[0:L, :] + y_s[L:R, :]
        s_new = (s_old * jnp.exp(tot) + dot_tn(u_s, _stack_heads(dec_end * ka))
                 + dot_tn(v_s, _stack_heads(dec_end * k)))
        s_ref[p] = s_new
        z_ref[:, sl] = _wkv_epilogue(y, r, k, v, g_ref[:, sl], lnw_ref[:, sl], lnb_ref[:, sl], rk_ref[:, sl],
                                     ones).astype(z_ref.dtype)

        @pl.when(c == pl.num_programs(2) - 1)
        def _():
            s_out_ref[0, 2 * p] = s_new[0:RWKV_HEAD, 0:RWKV_HEAD]
            s_out_ref[0, 2 * p + 1] = s_new[RWKV_HEAD:LANES, RWKV_HEAD:LANES]


def wkv_seq(r, k, v, kk, ka, ld, g, lnw, lnb, rk, n_seq, chunks, pairs):
    m, d = r.shape
    L = WKV_CHUNK
    w = pairs * LANES
    blk = pl.BlockSpec((L, w), lambda b, p, c: (b * chunks + c, p))
    vec = pl.BlockSpec((1, w), lambda b, p, c: (0, p))
    return pl.pallas_call(
        functools.partial(_wkv_seq_kernel, pairs=pairs),
        grid=(n_seq, d // w, chunks),
        in_specs=[blk] * 7 + [vec] * 3,
        out_specs=[blk, pl.BlockSpec((1, 2 * pairs, RWKV_HEAD, RWKV_HEAD), lambda b, p, c: (b, p, 0, 0))],
        out_shape=[jax.ShapeDtypeStruct((m, d), BF16),
                   jax.ShapeDtypeStruct((n_seq, RWKV_HEADS, RWKV_HEAD, RWKV_HEAD), F32)],
        scratch_shapes=[pltpu.VMEM((pairs, LANES, LANES), F32)],
        compiler_params=_params("parallel", "parallel", "arbitrary"),
        name="wkv_seq",
    )(r, k, v, kk, ka, ld, g, lnw, lnb, rk)


def _wkv_step_kernel(r_ref, k_ref, v_ref, kk_ref, ka_ref, ld_ref, g_ref, lnw_ref, lnb_ref, rk_ref, s_ref,
                     z_ref, s_out_ref):
    N = RWKV_HEAD
    eye = lax.broadcasted_iota(jnp.int32, (N, N), 0) == lax.broadcasted_iota(jnp.int32, (N, N), 1)
    for h in range(RWKV_HEADS):
        sl = slice(h * N, (h + 1) * N)
        r, k, v, kk, ka, ld, g = (ref[0, :, sl] for ref in (r_ref, k_ref, v_ref, kk_ref, ka_ref, ld_ref, g_ref))
        s_old = s_ref[0, h]
        sa = jnp.sum(s_old * (-kk), axis=-1, keepdims=True)
        v_col = jnp.sum(jnp.where(eye, jnp.broadcast_to(v, (N, N)), 0.0), axis=-1, keepdims=True)
        s_new = s_old * jnp.exp(ld) + sa * ka + v_col * k
        s_out_ref[0, h] = s_new
        y_col = jnp.sum(s_new * r, axis=-1, keepdims=True)
        y = jnp.sum(jnp.where(eye, jnp.broadcast_to(y_col, (N, N)), 0.0), axis=0, keepdims=True)
        yc = y - jnp.mean(y, axis=-1, keepdims=True)
        yn = yc * lax.rsqrt(jnp.mean(yc * yc, axis=-1, keepdims=True) + RWKV_GN_EPS)
        bonus = jnp.sum(r * k * rk_ref[:, sl], axis=-1, keepdims=True) * v
        z_ref[0, :, sl] = ((yn * lnw_ref[:, sl] + lnb_ref[:, sl] + bonus) * g).astype(z_ref.dtype)


def wkv_step(r, k, v, kk, ka, ld, g, lnw, lnb, rk, state):
    n, _, d = r.shape
    row = pl.BlockSpec((1, 1, d), lambda b: (b, 0, 0))
    vec = pl.BlockSpec((1, d), lambda b: (0, 0))
    st = pl.BlockSpec((1, RWKV_HEADS, RWKV_HEAD, RWKV_HEAD), lambda b: (b, 0, 0, 0))
    return pl.pallas_call(
        _wkv_step_kernel,
        grid=(n,),
        in_specs=[row] * 7 + [vec] * 3 + [st],
        out_specs=[row, st],
        out_shape=[jax.ShapeDtypeStruct((n, 1, d), BF16), jax.ShapeDtypeStruct(state.shape, F32)],
        compiler_params=_params("parallel"),
        name="wkv_step",
    )(r, k, v, kk, ka, ld, g, lnw, lnb, rk, state)


def _trunk(x, seq, states, p):
    depth = p['norm_mix'].shape[0]
    rows = x.shape[0]
    if seq is not None:
        n_seq, tp, lead = seq
        tm = _row_tile(tp, 1088)
        tm_small = _row_tile(tp, 544)
        tm_rwkv = _row_tile(tp, 272)
        tiles_small, tiles_rwkv = tp // tm_small, tp // tm_rwkv
        cos, sin = rot_table(tp, -lead, 1)
    else:
        ret_s, wkv_s, shift_s, conv_s = states
        tm = tm_small = rows
        tiles_small = 1
        lead = 0
        cos, sin = rot_table(8, PAST_LEN, 0)
    new_ret, new_wkv, new_shift, new_conv = [], [], [], []
    v_first = None
    for i in range(depth):
        j = i // 2
        gn = p['norm_mix'][i][None]
        if i % 2 == 0:
            gnw = p['ret_gn_w'][j][None]
            if seq is not None:
                proj = norm_proj(x, gn, p['ret_w_in'][j], tm, 1024, BF16)
                o, s = retention_prompt(proj, cos, sin, gnw, n_seq, tp // RET_CHUNK)
            else:
                proj = norm_proj(x, gn, p['ret_w_in'][j], tm, 1024, F32)
                o, s = retention_step(proj[:, None, :], cos, sin, gnw, ret_s[j])
                o = o[:, 0, :]
            new_ret.append(s)
            x = matmul_res(o, p['ret_w_out'][j], x, tm_small, tiles_small, lead)
        else:
            vecs = [p['rwkv_w0'][j], p['rwkv_a0'][j], p['rwkv_k_k'][j], p['rwkv_k_a'][j]]
            vecs.append(p['rwkv_v0'][j - 1] if j else jnp.zeros_like(vecs[0]))
            vecs = jnp.stack(vecs + [jnp.zeros_like(vecs[0])] * 3)
            wts = [p['rwkv_mu'][j], p['rwkv_w_rkv'][j], p['rwkv_w1'][j], p['rwkv_w2'][j], p['rwkv_a1'][j],
                   p['rwkv_a2'][j], p['rwkv_g1'][j], p['rwkv_g2'][j], vecs]
            vres = (v_first, p['rwkv_v1'][j - 1], p['rwkv_v2'][j - 1]) if j else None
            lnw, lnb = p['rwkv_ln_w'][j][None], p['rwkv_ln_b'][j][None]
            rk = p['rwkv_r_k'][j].reshape(1, D_MODEL)
            if seq is not None:
                r, k, v, kk, ka, ld, g, sh = rwkv_proj(x, gn, None, wts, vres, tm_rwkv, n_seq, tiles_rwkv)
                z, s = wkv_seq(r, k, v, kk, ka, ld, g, lnw, lnb, rk, n_seq, tp // WKV_CHUNK, 2)
                sh = sh[:, 0, :]
            else:
                r, k, v, kk, ka, ld, g, sh = rwkv_proj(x, gn, shift_s[j], wts, vres, tm, 1, 1)
                z, s = wkv_step(*(t[:, None, :] for t in (r, k, v, kk, ka, ld, g)), lnw, lnb, rk, wkv_s[j])
                z = z[:, 0, :]
            if v_first is None:
                v_first = v
            new_wkv.append(s)
            new_shift.append(sh)
            x = matmul_res(z, p['rwkv_w_o'][j], x, tm_small, tiles_small, lead)
        gf = p['norm_ffn'][i][None]
        cw, cb = p['ffn_conv_w'][i], p['ffn_conv_b'][i][None]
        if seq is not None:
            a, cbuf = ffn_up_seq(x, gf, p['ffn_w_ug'][i], cw, cb, tm_small, D_FF // 2, n_seq, tiles_small)
        else:
            a, n0, n1 = ffn_up_step(x, gf, p['ffn_w_ug'][i], cw, cb, conv_s[i].reshape(rows, 2 * D_FF), D_FF // 2)
            cbuf = jnp.stack([n0, n1], axis=1)
        new_conv.append(cbuf)
        x = matmul_res(a, p['ffn_w_d'][i], x, tm_small, tiles_small, lead)
    return x, jnp.stack(new_ret), jnp.stack(new_wkv), jnp.stack(new_shift), jnp.stack(new_conv)


def kernel(x_prompt, x_sample, state_ret, state_wkv, state_shift, state_conv, meta_tokens, norm_mix, norm_ffn, norm_final, ret_w_in, ret_gn_w, ret_w_out, rwkv_mu, rwkv_w_rkv, rwkv_w0, rwkv_w1, rwkv_w2, rwkv_a0, rwkv_a1, rwkv_a2, rwkv_v0, rwkv_v1, rwkv_v2, rwkv_g1, rwkv_g2, rwkv_k_k, rwkv_k_a, rwkv_r_k, rwkv_ln_w, rwkv_ln_b, rwkv_w_o, ffn_w_ug, ffn_conv_w, ffn_conv_b, ffn_w_d):
    bf = lambda w: w.astype(BF16)
    p = dict(norm_mix=norm_mix, norm_ffn=norm_ffn, ret_w_in=bf(ret_w_in), ret_gn_w=ret_gn_w,
             ret_w_out=bf(ret_w_out), rwkv_mu=rwkv_mu, rwkv_w_rkv=bf(rwkv_w_rkv), rwkv_w0=rwkv_w0,
             rwkv_w1=bf(rwkv_w1), rwkv_w2=bf(rwkv_w2), rwkv_a0=rwkv_a0, rwkv_a1=bf(rwkv_a1),
             rwkv_a2=bf(rwkv_a2), rwkv_v0=rwkv_v0, rwkv_v1=bf(rwkv_v1), rwkv_v2=bf(rwkv_v2),
             rwkv_g1=bf(rwkv_g1), rwkv_g2=bf(rwkv_g2), rwkv_k_k=rwkv_k_k, rwkv_k_a=rwkv_k_a,
             rwkv_r_k=rwkv_r_k, rwkv_ln_w=rwkv_ln_w, rwkv_ln_b=rwkv_ln_b, rwkv_w_o=bf(rwkv_w_o),
             ffn_w_ug=bf(ffn_w_ug), ffn_conv_w=ffn_conv_w, ffn_conv_b=ffn_conv_b, ffn_w_d=bf(ffn_w_d))
    B, S, D = x_prompt.shape
    lead = (-N_META) % RET_CHUNK
    tp = lead + N_META + S
    assert tp % RET_CHUNK == 0 and D == D_MODEL
    meta = jnp.broadcast_to(meta_tokens[None].astype(F32), (B, N_META, D))
    xp = jnp.concatenate([jnp.zeros((B, lead, D), F32), meta, x_prompt.astype(F32)], axis=1).reshape(B * tp, D)
    xp, p_ret, p_wkv, p_shift, p_conv = _trunk(xp, (B, tp, lead), None, p)
    skip = (lead + N_META) // RET_CHUNK
    y_prompt = final_norm(xp, norm_final[None], B, tp // RET_CHUNK, skip, RET_CHUNK).reshape(B, S, D)

    n_req = x_sample.shape[0]
    xs, s_ret, s_wkv, s_shift, s_conv = _trunk(
        x_sample.reshape(n_req, D).astype(F32), None, (state_ret, state_wkv, state_shift, state_conv), p)
    y_sample = final_norm(xs, norm_final[None], 1, 1, 0, n_req).reshape(n_req, 1, D)
    return (y_prompt.astype(x_prompt.dtype), y_sample.astype(x_sample.dtype),
            p_ret, p_wkv, p_shift, p_conv, s_ret, s_wkv, s_shift, s_conv)
```

```python
import functools
import math

import jax
import jax.numpy as jnp
from jax import lax
from jax.experimental import pallas as pl
from jax.experimental.pallas import tpu as pltpu

F32 = jnp.float32
BF16 = jnp.bfloat16

D_MODEL = 1024
N_META = 16
PAST_LEN = 16384
RET_HEADS = 4
RET_DK = D_MODEL // RET_HEADS
RET_DV = 2 * RET_DK
RET_VDIM = RET_HEADS * RET_DV
RET_CHUNK = 128
RWKV_HEAD = 64
RWKV_HEADS = D_MODEL // RWKV_HEAD
D_FF = 2816
RMS_EPS = 1e-6
RET_GN_EPS = 1e-5
RWKV_GN_EPS = 64e-5

LANES = 128
WKV_CHUNK = 64
VMEM_LIMIT = 56 * 1024 * 1024

NT_DIMS = (((1,), (1,)), ((), ()))
TN_DIMS = (((0,), (0,)), ((), ()))


def _params(*sem):
    return pltpu.CompilerParams(dimension_semantics=sem, vmem_limit_bytes=VMEM_LIMIT)


def _rms(x, g):
    return x * lax.rsqrt(jnp.mean(x * x, axis=-1, keepdims=True) + RMS_EPS) * g


def _sigmoid(x):
    return 1.0 / (1.0 + jnp.exp(-x))


def _row_tile(rows_per_seq, cap):
    best = None
    for t in range(16, min(rows_per_seq, cap) + 1, 16):
        if rows_per_seq % t == 0:
            best = t
    assert best is not None, rows_per_seq
    return best


def _rot_table_kernel(inv_ref, cos_ref, sin_ref, *, pos0, step):
    rows = lax.broadcasted_iota(jnp.int32, cos_ref.shape, 0)
    pos = (rows * step + pos0).astype(F32)
    ang = pos * inv_ref[...]
    cos_ref[...] = jnp.cos(ang)
    sin_ref[...] = jnp.sin(ang)


def rot_table(n_rows, pos0, step):
    half = RET_DK // 2
    inv = (1.0 / (10000.0 ** jnp.linspace(0.0, 1.0, half, dtype=F32))).reshape(1, half)
    return pl.pallas_call(
        functools.partial(_rot_table_kernel, pos0=pos0, step=step),
        out_shape=(jax.ShapeDtypeStruct((n_rows, half), F32),) * 2,
        name="rot_table",
    )(inv)


def _norm_proj_kernel(x_ref, g_ref, w_ref, o_ref, h_ref):
    @pl.when(pl.program_id(1) == 0)
    def _():
        h_ref[...] = _rms(x_ref[...], g_ref[...]).astype(BF16)

    o_ref[...] = jnp.dot(h_ref[...], w_ref[...], preferred_element_type=F32).astype(o_ref.dtype)


def norm_proj(x, g, w, tm, tn, out_dtype):
    m, d = x.shape
    n = w.shape[1]
    return pl.pallas_call(
        _norm_proj_kernel,
        grid=(m // tm, n // tn),
        in_specs=[pl.BlockSpec((tm, d), lambda i, j: (i, 0)),
                  pl.BlockSpec((1, d), lambda i, j: (0, 0)),
                  pl.BlockSpec((d, tn), lambda i, j: (0, j))],
        out_specs=pl.BlockSpec((tm, tn), lambda i, j: (i, j)),
        out_shape=jax.ShapeDtypeStruct((m, n), out_dtype),
        scratch_shapes=[pltpu.VMEM((tm, d), BF16)],
        compiler_params=_params("parallel", "arbitrary"),
        name="norm_proj",
    )(x, g, w)


def _matmul_res_kernel(a_ref, w_ref, x_ref, o_ref, *, tiles_per_seq, lead):
    y = x_ref[...] + jnp.dot(a_ref[...], w_ref[...], preferred_element_type=F32)
    if lead:
        tm = y.shape[0]
        row = (pl.program_id(0) % tiles_per_seq) * tm + lax.broadcasted_iota(jnp.int32, (tm, 1), 0)
        y = jnp.where(row >= lead, y, 0.0)
    o_ref[...] = y


def matmul_res(a, w, x, tm, tiles_per_seq, lead):
    m, k = a.shape
    n = w.shape[1]
    return pl.pallas_call(
        functools.partial(_matmul_res_kernel, tiles_per_seq=tiles_per_seq, lead=lead),
        grid=(m // tm,),
        in_specs=[pl.BlockSpec((tm, k), lambda i: (i, 0)),
                  pl.BlockSpec((k, n), lambda i: (0, 0)),
                  pl.BlockSpec((tm, n), lambda i: (i, 0))],
        out_specs=pl.BlockSpec((tm, n), lambda i: (i, 0)),
        out_shape=jax.ShapeDtypeStruct((m, n), F32),
        compiler_params=_params("parallel"),
        name="matmul_res",
    )(a, w, x)


def _final_norm_kernel(x_ref, g_ref, o_ref):
    o_ref[...] = _rms(x_ref[...], g_ref[...])


def final_norm(x, g, n_seq, chunks_per_seq, skip_chunks, rows):
    d = x.shape[1]
    keep = chunks_per_seq - skip_chunks
    return pl.pallas_call(
        _final_norm_kernel,
        grid=(n_seq, keep),
        in_specs=[pl.BlockSpec((rows, d), lambda b, c: (b * chunks_per_seq + skip_chunks + c, 0)),
                  pl.BlockSpec((1, d), lambda b, c: (0, 0))],
        out_specs=pl.BlockSpec((rows, d), lambda b, c: (b * keep + c, 0)),
        out_shape=jax.ShapeDtypeStruct((n_seq * keep * rows, d), F32),
        compiler_params=_params("parallel", "parallel"),
        name="final_norm",
    )(x, g)


def _rotate(x, cos, sin):
    half = RET_DK // 2
    x1, x2 = x[:, :half], x[:, half:]
    return jnp.concatenate([x1 * cos - x2 * sin, x1 * sin + x2 * cos], axis=1)


def _gated_head_norm(o, g, gnw):
    oc = o - jnp.mean(o, axis=-1, keepdims=True)
    on = oc * lax.rsqrt(jnp.mean(oc * oc, axis=-1, keepdims=True) + RET_GN_EPS)
    return (g * _sigmoid(g)) * (on * gnw)


def _retention_kernel(q_ref, k_ref, v_ref, g_ref, cos_ref, sin_ref, gnw_ref, lg_ref,
                      o_ref, s_out_ref, s_ref):
    c = pl.program_id(2)
    L = RET_CHUNK

    @pl.when(c == 0)
    def _():
        s_ref[...] = jnp.zeros_like(s_ref)

    lg = lg_ref[0:1, 0:1]
    cos, sin = cos_ref[...], sin_ref[...]
    q = _rotate(q_ref[...].astype(F32), cos, sin)
    k = _rotate(k_ref[...].astype(F32), cos, sin) * (RET_DK ** -0.5)
    v = v_ref[...]
    ri = lax.broadcasted_iota(jnp.int32, (L, 1), 0).astype(F32)
    ci = lax.broadcasted_iota(jnp.int32, (1, L), 1).astype(F32)
    rel = ri - ci
    dec = jnp.where(rel >= 0, jnp.exp(lg * jnp.maximum(rel, 0.0)), 0.0)
    qb = q.astype(BF16)
    s_old = s_ref[...]
    scores = lax.dot_general(qb, k.astype(BF16), NT_DIMS, preferred_element_type=F32) * dec
    o = (jnp.dot(scores.astype(BF16), v, preferred_element_type=F32)
         + jnp.exp(lg * (ri + 1.0)) * jnp.dot(qb, s_old.astype(BF16), preferred_element_type=F32))
    kd = (k * jnp.exp(lg * (L - 1.0 - ri))).astype(BF16)
    s_new = jnp.exp(lg * L) * s_old + lax.dot_general(kd, v, TN_DIMS, preferred_element_type=F32)
    s_ref[...] = s_new
    o_ref[...] = _gated_head_norm(o, g_ref[...].astype(F32), gnw_ref[...]).astype(o_ref.dtype)

    @pl.when(c == pl.num_programs(2) - 1)
    def _():
        s_out_ref[0, 0] = s_new


def _log_g(h):
    return math.log(1.0 - 2.0 ** (-5.0 - h))


def retention_prompt(proj, cos, sin, gnw, n_seq, chunks):
    L = RET_CHUNK
    m = proj.shape[0]
    qk_blocks = RET_HEADS
    v_blocks = 2 * RET_HEADS * RET_DK // RET_DV
    lg = jnp.repeat(jnp.asarray([_log_g(h) for h in range(RET_HEADS)], F32), 8)[:, None]
    lg = jnp.broadcast_to(lg, (8 * RET_HEADS, LANES))
    row = lambda b, h, c: b * chunks + c
    return pl.pallas_call(
        _retention_kernel,
        grid=(n_seq, RET_HEADS, chunks),
        in_specs=[pl.BlockSpec((L, RET_DK), lambda b, h, c: (row(b, h, c), h)),
                  pl.BlockSpec((L, RET_DK), lambda b, h, c: (row(b, h, c), qk_blocks + h)),
                  pl.BlockSpec((L, RET_DV), lambda b, h, c: (row(b, h, c), v_blocks + h)),
                  pl.BlockSpec((L, RET_DV), lambda b, h, c: (row(b, h, c), v_blocks + RET_HEADS + h)),
                  pl.BlockSpec((L, RET_DK // 2), lambda b, h, c: (c, 0)),
                  pl.BlockSpec((L, RET_DK // 2), lambda b, h, c: (c, 0)),
                  pl.BlockSpec((1, RET_DV), lambda b, h, c: (0, h)),
                  pl.BlockSpec((8, LANES), lambda b, h, c: (h, 0))],
        out_specs=[pl.BlockSpec((L, RET_DV), lambda b, h, c: (row(b, h, c), h)),
                   pl.BlockSpec((1, 1, RET_DK, RET_DV), lambda b, h, c: (b, h, 0, 0))],
        out_shape=[jax.ShapeDtypeStruct((m, RET_VDIM), BF16),
                   jax.ShapeDtypeStruct((n_seq, RET_HEADS, RET_DK, RET_DV), F32)],
        scratch_shapes=[pltpu.VMEM((RET_DK, RET_DV), F32)],
        compiler_params=_params("parallel", "parallel", "arbitrary"),
        name="retention_prompt",
    )(proj, proj, proj, proj, cos, sin, gnw, lg)


def _to_column(row, n):
    eye = lax.broadcasted_iota(jnp.int32, (n, n), 0) == lax.broadcasted_iota(jnp.int32, (n, n), 1)
    return jnp.sum(jnp.where(eye, jnp.broadcast_to(row, (n, n)), 0.0), axis=-1, keepdims=True)


def _retention_step_kernel(p_ref, cos_ref, sin_ref, gnw_ref, s_ref, o_ref, s_out_ref):
    cos, sin = cos_ref[0:1, :], sin_ref[0:1, :]
    qk = RET_HEADS * RET_DK
    for h in range(RET_HEADS):
        decay = math.exp(_log_g(h))
        q = _rotate(p_ref[0, :, h * RET_DK:(h + 1) * RET_DK], cos, sin)
        k = _rotate(p_ref[0, :, qk + h * RET_DK:qk + (h + 1) * RET_DK], cos, sin) * (RET_DK ** -0.5)
        v = p_ref[0, :, 2 * qk + h * RET_DV:2 * qk + (h + 1) * RET_DV]
        g = p_ref[0, :, 2 * qk + RET_VDIM + h * RET_DV:2 * qk + RET_VDIM + (h + 1) * RET_DV]
        s_old = s_ref[0, h]
        qs = jnp.sum(s_old * _to_column(q, RET_DK), axis=0, keepdims=True)
        o = jnp.sum(q * k, axis=-1, keepdims=True) * v + decay * qs
        s_out_ref[0, h] = decay * s_old + _to_column(k, RET_DK) * v
        o_ref[0, :, h * RET_DV:(h + 1) * RET_DV] = _gated_head_norm(
            o, g, gnw_ref[:, h * RET_DV:(h + 1) * RET_DV]).astype(o_ref.dtype)


def retention_step(proj, cos, sin, gnw, state):
    n = proj.shape[0]
    return pl.pallas_call(
        _retention_step_kernel,
        grid=(n,),
        in_specs=[pl.BlockSpec((1, 1, proj.shape[2]), lambda b: (b, 0, 0)),
                  pl.BlockSpec(cos.shape, lambda b: (0, 0)),
                  pl.BlockSpec(sin.shape, lambda b: (0, 0)),
                  pl.BlockSpec((1, RET_VDIM), lambda b: (0, 0)),
                  pl.BlockSpec((1, RET_HEADS, RET_DK, RET_DV), lambda b: (b, 0, 0, 0))],
        out_specs=[pl.BlockSpec((1, 1, RET_VDIM), lambda b: (b, 0, 0)),
                   pl.BlockSpec((1, RET_HEADS, RET_DK, RET_DV), lambda b: (b, 0, 0, 0))],
        out_shape=[jax.ShapeDtypeStruct((n, 1, RET_VDIM), BF16),
                   jax.ShapeDtypeStruct(state.shape, F32)],
        compiler_params=_params("parallel"),
        name="retention_step",
    )(proj, cos, sin, gnw, state)


def _conv_gate(gate, p1, p2, cw_ref, cb_ref):
    rows = lax.broadcasted_iota(jnp.int32, (gate.shape[0], 1), 0)
    g1 = jnp.where(rows == 0, p1, pltpu.roll(gate, 1, axis=0))
    g2 = jnp.where(rows == 0, p2, jnp.where(rows == 1, p1, pltpu.roll(gate, 2, axis=0)))
    return cb_ref[...] + g2 * cw_ref[0:1, :] + g1 * cw_ref[1:2, :] + gate * cw_ref[2:3, :]


def _ffn_up_seq_kernel(x_ref, g_ref, wu_ref, wg_ref, cw_ref, cb_ref, a_ref, nc_ref, h_ref, carry_ref,
                       *, tiles_per_seq):
    m, n = pl.program_id(0), pl.program_id(1)

    @pl.when(n == 0)
    def _():
        h_ref[...] = _rms(x_ref[...], g_ref[...]).astype(BF16)

    h = h_ref[...]
    u = jnp.dot(h, wu_ref[...], preferred_element_type=F32)
    gate = jnp.dot(h, wg_ref[...], preferred_element_type=F32)
    tm = gate.shape[0]

    @pl.when(m % tiles_per_seq == 0)
    def _():
        carry_ref[n] = jnp.zeros(carry_ref.shape[1:], F32)

    prev = carry_ref[n]
    conv = _conv_gate(gate, prev[1:2, :], prev[0:1, :], cw_ref, cb_ref)
    a_ref[...] = (conv * _sigmoid(conv) * u).astype(a_ref.dtype)
    last = gate[tm - 2:tm, :]
    carry_ref[n, 0:2, :] = last
    tn = gate.shape[1]
    for j in range(nc_ref.shape[2] // tn):
        @pl.when(n == j)
        def _():
            nc_ref[0, :, j * tn:(j + 1) * tn] = last


def ffn_up_seq(x, g, w_ug, cw, cb, tm, tn, n_seq, tiles_per_seq):
    m, d = x.shape
    nt = D_FF // tn
    return pl.pallas_call(
        functools.partial(_ffn_up_seq_kernel, tiles_per_seq=tiles_per_seq),
        grid=(m // tm, nt),
        in_specs=[pl.BlockSpec((tm, d), lambda i, j: (i, 0)),
                  pl.BlockSpec((1, d), lambda i, j: (0, 0)),
                  pl.BlockSpec((d, tn), lambda i, j: (0, j)),
                  pl.BlockSpec((d, tn), lambda i, j: (0, nt + j)),
                  pl.BlockSpec((3, tn), lambda i, j: (0, j)),
                  pl.BlockSpec((1, tn), lambda i, j: (0, j))],
        out_specs=[pl.BlockSpec((tm, tn), lambda i, j: (i, j)),
                   pl.BlockSpec((1, 2, D_FF), lambda i, j: (i // tiles_per_seq, 0, 0))],
        out_shape=[jax.ShapeDtypeStruct((m, D_FF), BF16),
                   jax.ShapeDtypeStruct((n_seq, 2, D_FF), F32)],
        scratch_shapes=[pltpu.VMEM((tm, d), BF16), pltpu.VMEM((nt, 8, tn), F32)],
        compiler_params=_params("arbitrary", "arbitrary"),
        name="ffn_up_seq",
    )(x, g, w_ug, w_ug, cw, cb)


def _ffn_up_step_kernel(x_ref, g_ref, wu_ref, wg_ref, cw_ref, cb_ref, b0_ref, b1_ref,
                        a_ref, n0_ref, n1_ref, h_ref):
    @pl.when(pl.program_id(0) == 0)
    def _():
        h_ref[...] = _rms(x_ref[...], g_ref[...]).astype(BF16)

    h = h_ref[...]
    u = jnp.dot(h, wu_ref[...], preferred_element_type=F32)
    gate = jnp.dot(h, wg_ref[...], preferred_element_type=F32)
    b1 = b1_ref[...]
    conv = cb_ref[...] + b0_ref[...] * cw_ref[0:1, :] + b1 * cw_ref[1:2, :] + gate * cw_ref[2:3, :]
    a_ref[...] = (conv * _sigmoid(conv) * u).astype(a_ref.dtype)
    n0_ref[...] = b1
    n1_ref[...] = gate


def ffn_up_step(x, g, w_ug, cw, cb, buf, tn):
    n_req, d = x.shape
    nt = D_FF // tn
    col = pl.BlockSpec((n_req, tn), lambda j: (0, j))
    return pl.pallas_call(
        _ffn_up_step_kernel,
        grid=(nt,),
        in_specs=[pl.BlockSpec((n_req, d), lambda j: (0, 0)),
                  pl.BlockSpec((1, d), lambda j: (0, 0)),
                  pl.BlockSpec((d, tn), lambda j: (0, j)),
                  pl.BlockSpec((d, tn), lambda j: (0, nt + j)),
                  pl.BlockSpec((3, tn), lambda j: (0, j)),
                  pl.BlockSpec((1, tn), lambda j: (0, j)),
                  col,
                  pl.BlockSpec((n_req, tn), lambda j: (0, nt + j))],
        out_specs=[col, col, col],
        out_shape=[jax.ShapeDtypeStruct((n_req, D_FF), BF16),
                   jax.ShapeDtypeStruct((n_req, D_FF), F32),
                   jax.ShapeDtypeStruct((n_req, D_FF), F32)],
        scratch_shapes=[pltpu.VMEM((n_req, d), BF16)],
        compiler_params=_params("arbitrary"),
        name="ffn_up_step",
    )(x, g, w_ug, w_ug, cw, cb, buf, buf)


def _pair_ones():
    r = lax.broadcasted_iota(jnp.int32, (LANES, LANES), 0) // RWKV_HEAD
    c = lax.broadcasted_iota(jnp.int32, (LANES, LANES), 1) // RWKV_HEAD
    return jnp.where(r == c, 1.0, 0.0).astype(BF16)


def _head_sum(x, ones):
    hi = x.astype(BF16)
    mid = (x - hi.astype(F32)).astype(BF16)
    lo = (x - hi.astype(F32) - mid.astype(F32)).astype(BF16)
    return (jnp.dot(hi, ones, preferred_element_type=F32) + jnp.dot(mid, ones, preferred_element_type=F32)
            + jnp.dot(lo, ones, preferred_element_type=F32))


def _rwkv_proj_body(h, hprev, mu_ref, wrkv_ref, w1_ref, w2_ref, a1_ref, a2_ref, g1_ref, g2_ref, vec_ref,
                    vres, outs):
    r_ref, k_ref, v_ref, kk_ref, ka_ref, ld_ref, g_ref = outs
    dx = hprev - h
    mix = lambda i: (h + dx * mu_ref[i:i + 1, :]).astype(BF16)
    xr, xw, xk, xv, xa, xg = (mix(i) for i in range(6))
    dot = lambda a, b: jnp.dot(a, b, preferred_element_type=F32)
    w0, a0, k_k, k_a = (vec_ref[i:i + 1, :] for i in range(4))
    r = dot(xr, wrkv_ref[0])
    k = dot(xk, wrkv_ref[1])
    v = dot(xv, wrkv_ref[2])
    z = w0 + dot(jnp.tanh(dot(xw, w1_ref[...])).astype(BF16), w2_ref[...])
    logw = -(jnp.maximum(-z, 0.0) + jnp.log(1.0 + jnp.exp(-jnp.abs(z)))) - 0.5
    ld_ref[...] = -jnp.exp(logw)
    a = _sigmoid(a0 + dot(dot(xa, a1_ref[...]).astype(BF16), a2_ref[...]))
    g_ref[...] = dot(_sigmoid(dot(xg, g1_ref[...])).astype(BF16), g2_ref[...])
    if vres is not None:
        vf_ref, v1_ref, v2_ref = vres
        v0 = vec_ref[4:5, :]
        v = v + (vf_ref[...] - v) * _sigmoid(v0 + dot(dot(xv, v1_ref[...]).astype(BF16), v2_ref[...]))
    kkr = k * k_k
    ones = _pair_ones()
    kk = jnp.concatenate(
        [kkr[:, t:t + LANES] * lax.rsqrt(jnp.maximum(
            _head_sum(kkr[:, t:t + LANES] * kkr[:, t:t + LANES], ones), 1e-12))
         for t in range(0, D_MODEL, LANES)], axis=1)
    r_ref[...] = r
    k_ref[...] = k * (1.0 + (a - 1.0) * k_a)
    v_ref[...] = v
    kk_ref[...] = kk
    ka_ref[...] = kk * a


def _rwkv_proj_seq_kernel(*refs, tiles_per_seq, has_vres):
    x_ref, gn_ref = refs[0], refs[1]
    weights = refs[2:11]
    n_in = 11 + (3 if has_vres else 0)
    vres = refs[11:14] if has_vres else None
    outs = refs[n_in:n_in + 7]
    shift_ref, carry_ref = refs[n_in + 7], refs[n_in + 8]
    m = pl.program_id(0)

    @pl.when(m % tiles_per_seq == 0)
    def _():
        carry_ref[...] = jnp.zeros_like(carry_ref)

    h = _rms(x_ref[...], gn_ref[...])
    tm = h.shape[0]
    rows = lax.broadcasted_iota(jnp.int32, (tm, 1), 0)
    hprev = jnp.where(rows == 0, carry_ref[0:1, :], pltpu.roll(h, 1, axis=0))
    _rwkv_proj_body(h, hprev, *weights, vres, outs)
    carry_ref[0:1, :] = h[tm - 1:tm, :]
    shift_ref[0] = h[tm - 1:tm, :]


def _rwkv_proj_step_kernel(*refs, has_vres):
    x_ref, gn_ref, prev_ref = refs[0], refs[1], refs[2]
    weights = refs[3:12]
    n_in = 12 + (3 if has_vres else 0)
    vres = refs[12:15] if has_vres else None
    outs = refs[n_in:n_in + 7]
    shift_ref = refs[n_in + 7]
    h = _rms(x_ref[...], gn_ref[...])
    _rwkv_proj_body(h, prev_ref[...], *weights, vres, outs)
    shift_ref[...] = h


def rwkv_proj(x, gn, prev, wts, vres, tm, n_seq, tiles_per_seq):
    m, d = x.shape
    full = lambda a: pl.BlockSpec(a.shape, lambda i, _n=a.ndim: (0,) * _n)
    rowblk = pl.BlockSpec((tm, d), lambda i: (i, 0))
    seq = prev is None
    ins = [x, gn] + ([] if seq else [prev]) + list(wts)
    specs = [rowblk, full(gn)] + ([] if seq else [rowblk]) + [full(a) for a in wts]
    if vres is not None:
        vf, v1, v2 = vres
        ins += [vf, v1, v2]
        specs += [rowblk, full(v1), full(v2)]
    out_shape = [jax.ShapeDtypeStruct((m, d), F32)] * 7
    out_specs = [rowblk] * 7
    if seq:
        out_shape.append(jax.ShapeDtypeStruct((n_seq, 1, d), F32))
        out_specs.append(pl.BlockSpec((1, 1, d), lambda i: (i // tiles_per_seq, 0, 0)))
        body = functools.partial(_rwkv_proj_seq_kernel, tiles_per_seq=tiles_per_seq, has_vres=vres is not None)
        scratch = [pltpu.VMEM((8, d), F32)]
    else:
        out_shape.append(jax.ShapeDtypeStruct((m, d), F32))
        out_specs.append(rowblk)
        body = functools.partial(_rwkv_proj_step_kernel, has_vres=vres is not None)
        scratch = []
    return pl.pallas_call(
        body,
        grid=(m // tm,),
        in_specs=specs,
        out_specs=out_specs,
        out_shape=out_shape,
        scratch_shapes=scratch,
        compiler_params=_params("arbitrary"),
        name="rwkv_proj_seq" if seq else "rwkv_proj_step",
    )(*ins)


def _wkv_epilogue(y, r, k, v, g, lnw, lnb, rk, ones):
    inv_n = 1.0 / RWKV_HEAD
    rows = y.shape[0]
    sums = _head_sum(jnp.concatenate([y, r * k * rk], axis=0), ones)
    yc = y - sums[0:rows] * inv_n
    yn = yc * lax.rsqrt(_head_sum(yc * yc, ones) * inv_n + RWKV_GN_EPS)
    return (yn * lnw + lnb + sums[rows:2 * rows] * v) * g


def _stack_heads(x):
    first = lax.broadcasted_iota(jnp.int32, (1, LANES), 1) < RWKV_HEAD
    return jnp.concatenate([jnp.where(first, x, 0.0), jnp.where(first, 0.0, x)], axis=0)


def _wkv_seq_kernel(r_ref, k_ref, v_ref, kk_ref, ka_ref, ld_ref, g_ref, lnw_ref, lnb_ref, rk_ref,
                    z_ref, s_out_ref, s_ref, *, pairs):
    c = pl.program_id(2)
    L = WKV_CHUNK
    R = 2 * L

    @pl.when(c == 0)
    def _():
        s_ref[...] = jnp.zeros_like(s_ref)

    dot = lambda a, b: jnp.dot(a.astype(BF16), b.astype(BF16), preferred_element_type=F32)
    dot_nt = lambda a, b: lax.dot_general(a.astype(BF16), b.astype(BF16), NT_DIMS, preferred_element_type=F32)
    dot_tn = lambda a, b: lax.dot_general(a.astype(BF16), b.astype(BF16), TN_DIMS, preferred_element_type=F32)
    ones = _pair_ones()
    ti = lax.broadcasted_iota(jnp.int32, (L, L), 0)
    tj = lax.broadcasted_iota(jnp.int32, (L, L), 1)
    tri = jnp.where(ti >= tj, 1.0, 0.0).astype(BF16)
    ri = lax.broadcasted_iota(jnp.int32, (R, R), 0)
    rj = lax.broadcasted_iota(jnp.int32, (R, R), 1)
    same = (ri // L) == (rj // L)
    lower = same & ((ri % L) > (rj % L))
    lower_eq = same & ((ri % L) >= (rj % L))
    eye = jnp.where(ri == rj, 1.0, 0.0)

    P = range(pairs)
    sls = [slice(p * LANES, (p + 1) * LANES) for p in P]
    ld = [ld_ref[:, sl] for sl in sls]
    ld_hi = [x.astype(BF16) for x in ld]
    ld_mid = [(x - h.astype(F32)).astype(BF16) for x, h in zip(ld, ld_hi)]
    ld_lo = [(x - h.astype(F32) - m.astype(F32)).astype(BF16) for x, h, m in zip(ld, ld_hi, ld_mid)]
    cum = [dot(tri, h) + dot(tri, m) + dot(tri, l) for h, m, l in zip(ld_hi, ld_mid, ld_lo)]
    tot = [x[L - 1:L, :] for x in cum]
    ar, bk, bk_end, v_s = [], [], [], []
    for p in P:
        r, k, v, kk, ka = (ref[:, sls[p]] for ref in (r_ref, k_ref, v_ref, kk_ref, ka_ref))
        dec_out = jnp.exp(-cum[p])
        dec_end = jnp.exp(tot[p] - cum[p])
        ar.append(jnp.concatenate([_stack_heads(jnp.exp(cum[p] - ld[p]) * (-kk)),
                                   _stack_heads(jnp.exp(cum[p]) * r)], axis=0).astype(BF16))
        bk.append(jnp.concatenate([_stack_heads(dec_out * ka), _stack_heads(dec_out * k)], axis=0).astype(BF16))
        bk_end.append(jnp.concatenate([_stack_heads(dec_end * ka), _stack_heads(dec_end * k)],
                                      axis=0).astype(BF16))
        v_s.append(_stack_heads(v))
    big = [dot_nt(ar[p], bk[p]) for p in P]
    n_ab = [jnp.where(lower, x[0:R, 0:R], 0.0) for x in big]
    a_ak = [jnp.where(lower, x[0:R, R:2 * R], 0.0).astype(BF16) for x in big]
    a_r = [jnp.concatenate([jnp.where(lower_eq, x[R:2 * R, 0:R], 0.0),
                            jnp.where(lower_eq, x[R:2 * R, R:2 * R], 0.0)], axis=1).astype(BF16) for x in big]
    inv = [eye + x for x in n_ab]
    pw = n_ab
    for _ in range(int(math.log2(L)) - 1):
        pw = [dot(x, x) for x in pw]
        inv = [x + dot(x, y) for x, y in zip(inv, pw)]
    s_old = [s_ref[p] for p in P]
    ars = [dot_nt(ar[p], s_old[p]) for p in P]
    akv = [dot(a_ak[p], v_s[p]) for p in P]
    u_s = [dot(inv[p], ars[p][0:R] + akv[p]) for p in P]
    uv = [jnp.concatenate([u_s[p], v_s[p]], axis=0).astype(BF16) for p in P]
    y_s = [ars[p][R:2 * R] + dot(a_r[p], uv[p]) for p in P]
    s_new = [s_old[p] * jnp.exp(tot[p]) + dot_tn(uv[p], bk_end[p]) for p in P]
    for p in P:
        s_ref[p] = s_new[p]
    for p in P:
        sl = sls[p]
        y = y_s[p][0:L, :] + y_s[p][L:R, :]
        z_ref[:, sl] = _wkv_epilogue(y, r_ref[:, sl], k_ref[:, sl], v_ref[:, sl], g_ref[:, sl], lnw_ref[:, sl],
                                     lnb_ref[:, sl], rk_ref[:, sl], ones).astype(z_ref.dtype)

    @pl.when(c == pl.num_programs(2) - 1)
    def _():
        for p in P:
            s_out_ref[0, 2 * p] = s_new[p][0:RWKV_HEAD, 0:RWKV_HEAD]
            s_out_ref[0, 2 * p + 1] = s_new[p][RWKV_HEAD:LANES, RWKV_HEAD:LANES]


def wkv_seq(r, k, v, kk, ka, ld, g, lnw, lnb, rk, n_seq, chunks, pairs):
    m, d = r.shape
    L = WKV_CHUNK
    w = pairs * LANES
    blk = pl.BlockSpec((L, w), lambda b, p, c: (b * chunks + c, p))
    vec = pl.BlockSpec((1, w), lambda b, p, c: (0, p))
    return pl.pallas_call(
        functools.partial(_wkv_seq_kernel, pairs=pairs),
        grid=(n_seq, d // w, chunks),
        in_specs=[blk] * 7 + [vec] * 3,
        out_specs=[blk, pl.BlockSpec((1, 2 * pairs, RWKV_HEAD, RWKV_HEAD), lambda b, p, c: (b, p, 0, 0))],
        out_shape=[jax.ShapeDtypeStruct((m, d), BF16),
                   jax.ShapeDtypeStruct((n_seq, RWKV_HEADS, RWKV_HEAD, RWKV_HEAD), F32)],
        scratch_shapes=[pltpu.VMEM((pairs, LANES, LANES), F32)],
        compiler_params=_params("parallel", "parallel", "arbitrary"),
        name="wkv_seq",
    )(r, k, v, kk, ka, ld, g, lnw, lnb, rk)


def _wkv_step_kernel(r_ref, k_ref, v_ref, kk_ref, ka_ref, ld_ref, g_ref, lnw_ref, lnb_ref, rk_ref, s_ref,
                     z_ref, s_out_ref):
    N = RWKV_HEAD
    eye = lax.broadcasted_iota(jnp.int32, (N, N), 0) == lax.broadcasted_iota(jnp.int32, (N, N), 1)
    for h in range(RWKV_HEADS):
        sl = slice(h * N, (h + 1) * N)
        r, k, v, kk, ka, ld, g = (ref[0, :, sl] for ref in (r_ref, k_ref, v_ref, kk_ref, ka_ref, ld_ref, g_ref))
        s_old = s_ref[0, h]
        sa = jnp.sum(s_old * (-kk), axis=-1, keepdims=True)
        v_col = jnp.sum(jnp.where(eye, jnp.broadcast_to(v, (N, N)), 0.0), axis=-1, keepdims=True)
        s_new = s_old * jnp.exp(ld) + sa * ka + v_col * k
        s_out_ref[0, h] = s_new
        y_col = jnp.sum(s_new * r, axis=-1, keepdims=True)
        y = jnp.sum(jnp.where(eye, jnp.broadcast_to(y_col, (N, N)), 0.0), axis=0, keepdims=True)
        yc = y - jnp.mean(y, axis=-1, keepdims=True)
        yn = yc * lax.rsqrt(jnp.mean(yc * yc, axis=-1, keepdims=True) + RWKV_GN_EPS)
        bonus = jnp.sum(r * k * rk_ref[:, sl], axis=-1, keepdims=True) * v
        z_ref[0, :, sl] = ((yn * lnw_ref[:, sl] + lnb_ref[:, sl] + bonus) * g).astype(z_ref.dtype)


def wkv_step(r, k, v, kk, ka, ld, g, lnw, lnb, rk, state):
    n, _, d = r.shape
    row = pl.BlockSpec((1, 1, d), lambda b: (b, 0, 0))
    vec = pl.BlockSpec((1, d), lambda b: (0, 0))
    st = pl.BlockSpec((1, RWKV_HEADS, RWKV_HEAD, RWKV_HEAD), lambda b: (b, 0, 0, 0))
    return pl.pallas_call(
        _wkv_step_kernel,
        grid=(n,),
        in_specs=[row] * 7 + [vec] * 3 + [st],
        out_specs=[row, st],
        out_shape=[jax.ShapeDtypeStruct((n, 1, d), BF16), jax.ShapeDtypeStruct(state.shape, F32)],
        compiler_params=_params("parallel"),
        name="wkv_step",
    )(r, k, v, kk, ka, ld, g, lnw, lnb, rk, state)


def _trunk(x, seq, states, p):
    depth = p['norm_mix'].shape[0]
    rows = x.shape[0]
    if seq is not None:
        n_seq, tp, lead = seq
        tm = _row_tile(tp, 1088)
        tm_small = _row_tile(tp, 544)
        tm_rwkv = _row_tile(tp, 272)
        tiles_small, tiles_rwkv = tp // tm_small, tp // tm_rwkv
        cos, sin = rot_table(tp, -lead, 1)
    else:
        ret_s, wkv_s, shift_s, conv_s = states
        tm = tm_small = rows
        tiles_small = 1
        lead = 0
        cos, sin = rot_table(8, PAST_LEN, 0)
    new_ret, new_wkv, new_shift, new_conv = [], [], [], []
    v_first = None
    for i in range(depth):
        j = i // 2
        gn = p['norm_mix'][i][None]
        if i % 2 == 0:
            gnw = p['ret_gn_w'][j][None]
            if seq is not None:
                proj = norm_proj(x, gn, p['ret_w_in'][j], tm, 1024, BF16)
                o, s = retention_prompt(proj, cos, sin, gnw, n_seq, tp // RET_CHUNK)
            else:
                proj = norm_proj(x, gn, p['ret_w_in'][j], tm, 1024, F32)
                o, s = retention_step(proj[:, None, :], cos, sin, gnw, ret_s[j])
                o = o[:, 0, :]
            new_ret.append(s)
            x = matmul_res(o, p['ret_w_out'][j], x, tm_small, tiles_small, lead)
        else:
            vecs = [p['rwkv_w0'][j], p['rwkv_a0'][j], p['rwkv_k_k'][j], p['rwkv_k_a'][j]]
            vecs.append(p['rwkv_v0'][j - 1] if j else jnp.zeros_like(vecs[0]))
            vecs = jnp.stack(vecs + [jnp.zeros_like(vecs[0])] * 3)
            wts = [p['rwkv_mu'][j], p['rwkv_w_rkv'][j], p['rwkv_w1'][j], p['rwkv_w2'][j], p['rwkv_a1'][j],
                   p['rwkv_a2'][j], p['rwkv_g1'][j], p['rwkv_g2'][j], vecs]
            vres = (v_first, p['rwkv_v1'][j - 1], p['rwkv_v2'][j - 1]) if j else None
            lnw, lnb = p['rwkv_ln_w'][j][None], p['rwkv_ln_b'][j][None]
            rk = p['rwkv_r_k'][j].reshape(1, D_MODEL)
            if seq is not None:
                r, k, v, kk, ka, ld, g, sh = rwkv_proj(x, gn, None, wts, vres, tm_rwkv, n_seq, tiles_rwkv)
                z, s = wkv_seq(r, k, v, kk, ka, ld, g, lnw, lnb, rk, n_seq, tp // WKV_CHUNK, D_MODEL // LANES)
                sh = sh[:, 0, :]
            else:
                r, k, v, kk, ka, ld, g, sh = rwkv_proj(x, gn, shift_s[j], wts, vres, tm, 1, 1)
                z, s = wkv_step(*(t[:, None, :] for t in (r, k, v, kk, ka, ld, g)), lnw, lnb, rk, wkv_s[j])
                z = z[:, 0, :]
            if v_first is None:
                v_first = v
            new_wkv.append(s)
            new_shift.append(sh)
            x = matmul_res(z, p['rwkv_w_o'][j], x, tm_small, tiles_small, lead)
        gf = p['norm_ffn'][i][None]
        cw, cb = p['ffn_conv_w'][i], p['ffn_conv_b'][i][None]
        if seq is not None:
            a, cbuf = ffn_up_seq(x, gf, p['ffn_w_ug'][i], cw, cb, tm_small, D_FF // 2, n_seq, tiles_small)
        else:
            a, n0, n1 = ffn_up_step(x, gf, p['ffn_w_ug'][i], cw, cb, conv_s[i].reshape(rows, 2 * D_FF), D_FF // 2)
            cbuf = jnp.stack([n0, n1], axis=1)
        new_conv.append(cbuf)
        x = matmul_res(a, p['ffn_w_d'][i], x, tm_small, tiles_small, lead)
    return x, jnp.stack(new_ret), jnp.stack(new_wkv), jnp.stack(new_shift), jnp.stack(new_conv)


def kernel(x_prompt, x_sample, state_ret, state_wkv, state_shift, state_conv, meta_tokens, norm_mix, norm_ffn, norm_final, ret_w_in, ret_gn_w, ret_w_out, rwkv_mu, rwkv_w_rkv, rwkv_w0, rwkv_w1, rwkv_w2, rwkv_a0, rwkv_a1, rwkv_a2, rwkv_v0, rwkv_v1, rwkv_v2, rwkv_g1, rwkv_g2, rwkv_k_k, rwkv_k_a, rwkv_r_k, rwkv_ln_w, rwkv_ln_b, rwkv_w_o, ffn_w_ug, ffn_conv_w, ffn_conv_b, ffn_w_d):
    bf = lambda w: w.astype(BF16)
    p = dict(norm_mix=norm_mix, norm_ffn=norm_ffn, ret_w_in=bf(ret_w_in), ret_gn_w=ret_gn_w,
             ret_w_out=bf(ret_w_out), rwkv_mu=rwkv_mu, rwkv_w_rkv=bf(rwkv_w_rkv), rwkv_w0=rwkv_w0,
             rwkv_w1=bf(rwkv_w1), rwkv_w2=bf(rwkv_w2), rwkv_a0=rwkv_a0, rwkv_a1=bf(rwkv_a1),
             rwkv_a2=bf(rwkv_a2), rwkv_v0=rwkv_v0, rwkv_v1=bf(rwkv_v1), rwkv_v2=bf(rwkv_v2),
             rwkv_g1=bf(rwkv_g1), rwkv_g2=bf(rwkv_g2), rwkv_k_k=rwkv_k_k, rwkv_k_a=rwkv_k_a,
             rwkv_r_k=rwkv_r_k, rwkv_ln_w=rwkv_ln_w, rwkv_ln_b=rwkv_ln_b, rwkv_w_o=bf(rwkv_w_o),
             ffn_w_ug=bf(ffn_w_ug), ffn_conv_w=ffn_conv_w, ffn_conv_b=ffn_conv_b, ffn_w_d=bf(ffn_w_d))
    B, S, D = x_prompt.shape
    lead = (-N_META) % RET_CHUNK
    tp = lead + N_META + S
    assert tp % RET_CHUNK == 0 and D == D_MODEL
    meta = jnp.broadcast_to(meta_tokens[None].astype(F32), (B, N_META, D))
    xp = jnp.concatenate([jnp.zeros((B, lead, D), F32), meta, x_prompt.astype(F32)], axis=1).reshape(B * tp, D)
    xp, p_ret, p_wkv, p_shift, p_conv = _trunk(xp, (B, tp, lead), None, p)
    skip = (lead + N_META) // RET_CHUNK
    y_prompt = final_norm(xp, norm_final[None], B, tp // RET_CHUNK, skip, RET_CHUNK).reshape(B, S, D)

    n_req = x_sample.shape[0]
    xs, s_ret, s_wkv, s_shift, s_conv = _trunk(
        x_sample.reshape(n_req, D).astype(F32), None, (state_ret, state_wkv, state_shift, state_conv), p)
    y_sample = final_norm(xs, norm_final[None], 1, 1, 0, n_req).reshape(n_req, 1, D)
    return (y_prompt.astype(x_prompt.dtype), y_sample.astype(x_sample.dtype),
            p_ret, p_wkv, p_shift, p_conv, s_ret, s_wkv, s_shift, s_conv)
```

```python
import functools
import math

import jax
import jax.numpy as jnp
from jax import lax
from jax.experimental import pallas as pl
from jax.experimental.pallas import tpu as pltpu

F32 = jnp.float32
BF16 = jnp.bfloat16

D_MODEL = 1024
N_META = 16
PAST_LEN = 16384
RET_HEADS = 4
RET_DK = D_MODEL // RET_HEADS
RET_DV = 2 * RET_DK
RET_VDIM = RET_HEADS * RET_DV
RET_CHUNK = 128
RWKV_HEAD = 64
RWKV_HEADS = D_MODEL // RWKV_HEAD
D_FF = 2816
RMS_EPS = 1e-6
RET_GN_EPS = 1e-5
RWKV_GN_EPS = 64e-5

LANES = 128
WKV_CHUNK = 64
WKV_GROUP = 8
VMEM_LIMIT = 56 * 1024 * 1024

NT_DIMS = (((1,), (1,)), ((), ()))
TN_DIMS = (((0,), (0,)), ((), ()))


def _params(*sem):
    return pltpu.CompilerParams(dimension_semantics=sem, vmem_limit_bytes=VMEM_LIMIT)


def _rms(x, g):
    return x * lax.rsqrt(jnp.mean(x * x, axis=-1, keepdims=True) + RMS_EPS) * g


def _sigmoid(x):
    return 1.0 / (1.0 + jnp.exp(-x))


def _row_tile(rows_per_seq, cap):
    best = None
    for t in range(16, min(rows_per_seq, cap) + 1, 16):
        if rows_per_seq % t == 0:
            best = t
    assert best is not None, rows_per_seq
    return best


def _rot_table_kernel(inv_ref, cos_ref, sin_ref, *, pos0, step):
    rows = lax.broadcasted_iota(jnp.int32, cos_ref.shape, 0)
    pos = (rows * step + pos0).astype(F32)
    ang = pos * inv_ref[...]
    cos_ref[...] = jnp.cos(ang)
    sin_ref[...] = jnp.sin(ang)


def rot_table(n_rows, pos0, step):
    half = RET_DK // 2
    inv = (1.0 / (10000.0 ** jnp.linspace(0.0, 1.0, half, dtype=F32))).reshape(1, half)
    return pl.pallas_call(
        functools.partial(_rot_table_kernel, pos0=pos0, step=step),
        out_shape=(jax.ShapeDtypeStruct((n_rows, half), F32),) * 2,
        name="rot_table",
    )(inv)


def _norm_proj_kernel(x_ref, g_ref, w_ref, o_ref, h_ref):
    @pl.when(pl.program_id(1) == 0)
    def _():
        h_ref[...] = _rms(x_ref[...], g_ref[...]).astype(BF16)

    o_ref[...] = jnp.dot(h_ref[...], w_ref[...], preferred_element_type=F32).astype(o_ref.dtype)


def norm_proj(x, g, w, tm, tn, out_dtype):
    m, d = x.shape
    n = w.shape[1]
    return pl.pallas_call(
        _norm_proj_kernel,
        grid=(m // tm, n // tn),
        in_specs=[pl.BlockSpec((tm, d), lambda i, j: (i, 0)),
                  pl.BlockSpec((1, d), lambda i, j: (0, 0)),
                  pl.BlockSpec((d, tn), lambda i, j: (0, j))],
        out_specs=pl.BlockSpec((tm, tn), lambda i, j: (i, j)),
        out_shape=jax.ShapeDtypeStruct((m, n), out_dtype),
        scratch_shapes=[pltpu.VMEM((tm, d), BF16)],
        compiler_params=_params("parallel", "arbitrary"),
        name="norm_proj",
    )(x, g, w)


def _matmul_res_kernel(a_ref, w_ref, x_ref, o_ref, *, tiles_per_seq, lead):
    y = x_ref[...] + jnp.dot(a_ref[...].astype(BF16), w_ref[...], preferred_element_type=F32)
    if lead:
        tm = y.shape[0]
        row = (pl.program_id(0) % tiles_per_seq) * tm + lax.broadcasted_iota(jnp.int32, (tm, 1), 0)
        y = jnp.where(row >= lead, y, 0.0)
    o_ref[...] = y


def matmul_res(a, w, x, tm, tiles_per_seq, lead):
    m, k = a.shape
    n = w.shape[1]
    return pl.pallas_call(
        functools.partial(_matmul_res_kernel, tiles_per_seq=tiles_per_seq, lead=lead),
        grid=(m // tm,),
        in_specs=[pl.BlockSpec((tm, k), lambda i: (i, 0)),
                  pl.BlockSpec((k, n), lambda i: (0, 0)),
                  pl.BlockSpec((tm, n), lambda i: (i, 0))],
        out_specs=pl.BlockSpec((tm, n), lambda i: (i, 0)),
        out_shape=jax.ShapeDtypeStruct((m, n), F32),
        compiler_params=_params("parallel"),
        name="matmul_res",
    )(a, w, x)


def _final_norm_kernel(x_ref, g_ref, o_ref):
    o_ref[...] = _rms(x_ref[...], g_ref[...])


def final_norm(x, g, n_seq, chunks_per_seq, skip_chunks, rows):
    d = x.shape[1]
    keep = chunks_per_seq - skip_chunks
    return pl.pallas_call(
        _final_norm_kernel,
        grid=(n_seq, keep),
        in_specs=[pl.BlockSpec((rows, d), lambda b, c: (b * chunks_per_seq + skip_chunks + c, 0)),
                  pl.BlockSpec((1, d), lambda b, c: (0, 0))],
        out_specs=pl.BlockSpec((rows, d), lambda b, c: (b * keep + c, 0)),
        out_shape=jax.ShapeDtypeStruct((n_seq * keep * rows, d), F32),
        compiler_params=_params("parallel", "parallel"),
        name="final_norm",
    )(x, g)


def _rotate(x, cos, sin):
    half = RET_DK // 2
    x1, x2 = x[:, :half], x[:, half:]
    return jnp.concatenate([x1 * cos - x2 * sin, x1 * sin + x2 * cos], axis=1)


def _gated_head_norm(o, g, gnw):
    oc = o - jnp.mean(o, axis=-1, keepdims=True)
    on = oc * lax.rsqrt(jnp.mean(oc * oc, axis=-1, keepdims=True) + RET_GN_EPS)
    return (g * _sigmoid(g)) * (on * gnw)


def _log_g(h):
    return math.log(1.0 - 2.0 ** (-5.0 - h))


def _layer_slab(acc, ins, specs, out_index):
    if acc is None:
        return {}
    ins.append(acc)
    specs.append(pl.BlockSpec(memory_space=pl.ANY))
    return {len(ins) - 1: out_index}


def _retention_kernel(*refs, has_acc):
    p_ref, cos_ref, sin_ref, gnw_ref = refs[:4]
    o_ref, s_out_ref, s_ref = refs[4 + has_acc:]
    c = pl.program_id(1)
    L = RET_CHUNK
    heads = range(RET_HEADS)
    qk = RET_HEADS * RET_DK

    @pl.when(c == 0)
    def _():
        s_ref[...] = jnp.zeros_like(s_ref)

    dot = lambda a, b: jnp.dot(a, b, preferred_element_type=F32)
    cos, sin = cos_ref[...], sin_ref[...]
    ri = lax.broadcasted_iota(jnp.int32, (L, 1), 0).astype(F32)
    ci = lax.broadcasted_iota(jnp.int32, (1, L), 1).astype(F32)
    rel = ri - ci
    lg = [_log_g(h) for h in heads]
    q = [_rotate(p_ref[:, h * RET_DK:(h + 1) * RET_DK].astype(F32), cos, sin).astype(BF16) for h in heads]
    kf = [_rotate(p_ref[:, qk + h * RET_DK:qk + (h + 1) * RET_DK].astype(F32), cos, sin) * (RET_DK ** -0.5)
          for h in heads]
    v = [p_ref[:, 2 * qk + h * RET_DV:2 * qk + (h + 1) * RET_DV] for h in heads]
    s_old = [s_ref[h] for h in heads]
    scores = [lax.dot_general(q[h], kf[h].astype(BF16), NT_DIMS, preferred_element_type=F32)
              * jnp.where(rel >= 0, jnp.exp(lg[h] * jnp.maximum(rel, 0.0)), 0.0) for h in heads]
    qs = [dot(q[h], s_old[h].astype(BF16)) for h in heads]
    o = [dot(scores[h].astype(BF16), v[h]) + jnp.exp(lg[h] * (ri + 1.0)) * qs[h] for h in heads]
    kd = [(kf[h] * jnp.exp(lg[h] * (L - 1.0 - ri))).astype(BF16) for h in heads]
    s_new = [math.exp(lg[h] * L) * s_old[h] + lax.dot_general(kd[h], v[h], TN_DIMS, preferred_element_type=F32)
             for h in heads]
    for h in heads:
        s_ref[h] = s_new[h]
    g0 = 2 * qk + RET_VDIM
    for h in heads:
        sl = slice(h * RET_DV, (h + 1) * RET_DV)
        g = p_ref[:, g0 + h * RET_DV:g0 + (h + 1) * RET_DV].astype(F32)
        o_ref[:, sl] = _gated_head_norm(o[h], g, gnw_ref[:, sl]).astype(o_ref.dtype)

    @pl.when(c == pl.num_programs(1) - 1)
    def _():
        for h in heads:
            s_out_ref[0, 0, h] = s_new[h]


def retention_prompt(proj, cos, sin, gnw, n_seq, chunks, layer, n_layers, acc):
    L = RET_CHUNK
    m, width = proj.shape
    ins = [proj, cos, sin, gnw]
    specs = [pl.BlockSpec((L, width), lambda b, c: (b * chunks + c, 0)),
             pl.BlockSpec((L, RET_DK // 2), lambda b, c: (c, 0)),
             pl.BlockSpec((L, RET_DK // 2), lambda b, c: (c, 0)),
             pl.BlockSpec((1, RET_VDIM), lambda b, c: (0, 0))]
    aliases = _layer_slab(acc, ins, specs, 1)
    return pl.pallas_call(
        functools.partial(_retention_kernel, has_acc=acc is not None),
        grid=(n_seq, chunks),
        in_specs=specs,
        out_specs=[pl.BlockSpec((L, RET_VDIM), lambda b, c: (b * chunks + c, 0)),
                   pl.BlockSpec((1, 1, RET_HEADS, RET_DK, RET_DV), lambda b, c: (layer, b, 0, 0, 0))],
        out_shape=[jax.ShapeDtypeStruct((m, RET_VDIM), BF16),
                   jax.ShapeDtypeStruct((n_layers, n_seq, RET_HEADS, RET_DK, RET_DV), F32)],
        scratch_shapes=[pltpu.VMEM((RET_HEADS, RET_DK, RET_DV), F32)],
        input_output_aliases=aliases,
        compiler_params=_params("parallel", "arbitrary"),
        name="retention_prompt",
    )(*ins)


def _to_column(row, n):
    eye = lax.broadcasted_iota(jnp.int32, (n, n), 0) == lax.broadcasted_iota(jnp.int32, (n, n), 1)
    return jnp.sum(jnp.where(eye, jnp.broadcast_to(row, (n, n)), 0.0), axis=-1, keepdims=True)


def _retention_step_kernel(*refs, has_acc):
    p_ref, cos_ref, sin_ref, gnw_ref, s_ref = refs[:5]
    o_ref, s_out_ref = refs[5 + has_acc:]
    cos, sin = cos_ref[0:1, :], sin_ref[0:1, :]
    qk = RET_HEADS * RET_DK
    for h in range(RET_HEADS):
        decay = math.exp(_log_g(h))
        q = _rotate(p_ref[0, :, h * RET_DK:(h + 1) * RET_DK], cos, sin)
        k = _rotate(p_ref[0, :, qk + h * RET_DK:qk + (h + 1) * RET_DK], cos, sin) * (RET_DK ** -0.5)
        v = p_ref[0, :, 2 * qk + h * RET_DV:2 * qk + (h + 1) * RET_DV]
        g = p_ref[0, :, 2 * qk + RET_VDIM + h * RET_DV:2 * qk + RET_VDIM + (h + 1) * RET_DV]
        s_old = s_ref[0, 0, h]
        qs = jnp.sum(s_old * _to_column(q, RET_DK), axis=0, keepdims=True)
        o = jnp.sum(q * k, axis=-1, keepdims=True) * v + decay * qs
        s_out_ref[0, 0, h] = decay * s_old + _to_column(k, RET_DK) * v
        o_ref[0, :, h * RET_DV:(h + 1) * RET_DV] = _gated_head_norm(
            o, g, gnw_ref[:, h * RET_DV:(h + 1) * RET_DV]).astype(o_ref.dtype)


def retention_step(proj, cos, sin, gnw, states, layer, acc):
    n = proj.shape[0]
    st = pl.BlockSpec((1, 1, RET_HEADS, RET_DK, RET_DV), lambda b: (layer, b, 0, 0, 0))
    ins = [proj, cos, sin, gnw, states]
    specs = [pl.BlockSpec((1, 1, proj.shape[2]), lambda b: (b, 0, 0)),
             pl.BlockSpec(cos.shape, lambda b: (0, 0)),
             pl.BlockSpec(sin.shape, lambda b: (0, 0)),
             pl.BlockSpec((1, RET_VDIM), lambda b: (0, 0)),
             st]
    aliases = _layer_slab(acc, ins, specs, 1)
    return pl.pallas_call(
        functools.partial(_retention_step_kernel, has_acc=acc is not None),
        grid=(n,),
        in_specs=specs,
        out_specs=[pl.BlockSpec((1, 1, RET_VDIM), lambda b: (b, 0, 0)), st],
        out_shape=[jax.ShapeDtypeStruct((n, 1, RET_VDIM), BF16),
                   jax.ShapeDtypeStruct(states.shape, F32)],
        input_output_aliases=aliases,
        compiler_params=_params("parallel"),
        name="retention_step",
    )(*ins)


FFN_COLS = 256


def _conv_gate(gate, p1, p2, cw, cb):
    rows = lax.broadcasted_iota(jnp.int32, (gate.shape[0], 1), 0)
    g1 = jnp.where(rows == 0, p1, pltpu.roll(gate, 1, axis=0))
    g2 = jnp.where(rows == 0, p2, jnp.where(rows == 1, p1, pltpu.roll(gate, 2, axis=0)))
    return cb + g2 * cw[0:1, :] + g1 * cw[1:2, :] + gate * cw[2:3, :]


def _ffn_up_seq_kernel(x_ref, g_ref, w_ref, cw_ref, cb_ref, a_ref, nc_ref, carry_ref, *, tiles_per_seq):
    @pl.when(pl.program_id(0) % tiles_per_seq == 0)
    def _():
        carry_ref[...] = jnp.zeros_like(carry_ref)

    h = _rms(x_ref[...], g_ref[...]).astype(BF16)
    tm = h.shape[0]
    for j in range(D_FF // FFN_COLS):
        sl = slice(j * FFN_COLS, (j + 1) * FFN_COLS)
        u = jnp.dot(h, w_ref[:, sl], preferred_element_type=F32)
        gate = jnp.dot(h, w_ref[:, D_FF + j * FFN_COLS:D_FF + (j + 1) * FFN_COLS], preferred_element_type=F32)
        conv = _conv_gate(gate, carry_ref[1:2, sl], carry_ref[0:1, sl], cw_ref[:, sl], cb_ref[:, sl])
        a_ref[:, sl] = (conv * _sigmoid(conv) * u).astype(a_ref.dtype)
        last = gate[tm - 2:tm, :]
        carry_ref[0:2, sl] = last
        nc_ref[0, :, sl] = last


def ffn_up_seq(x, g, w_ug, cw, cb, tm, n_seq, tiles_per_seq):
    m, d = x.shape
    return pl.pallas_call(
        functools.partial(_ffn_up_seq_kernel, tiles_per_seq=tiles_per_seq),
        grid=(m // tm,),
        in_specs=[pl.BlockSpec((tm, d), lambda i: (i, 0)),
                  pl.BlockSpec((1, d), lambda i: (0, 0)),
                  pl.BlockSpec(w_ug.shape, lambda i: (0, 0)),
                  pl.BlockSpec(cw.shape, lambda i: (0, 0)),
                  pl.BlockSpec(cb.shape, lambda i: (0, 0))],
        out_specs=[pl.BlockSpec((tm, D_FF), lambda i: (i, 0)),
                   pl.BlockSpec((1, 2, D_FF), lambda i: (i // tiles_per_seq, 0, 0))],
        out_shape=[jax.ShapeDtypeStruct((m, D_FF), BF16),
                   jax.ShapeDtypeStruct((n_seq, 2, D_FF), F32)],
        scratch_shapes=[pltpu.VMEM((8, D_FF), F32)],
        compiler_params=_params("arbitrary"),
        name="ffn_up_seq",
    )(x, g, w_ug, cw, cb)


def _ffn_up_step_kernel(x_ref, g_ref, wu_ref, wg_ref, cw_ref, cb_ref, b0_ref, b1_ref,
                        a_ref, n0_ref, n1_ref, h_ref):
    @pl.when(pl.program_id(0) == 0)
    def _():
        h_ref[...] = _rms(x_ref[...], g_ref[...]).astype(BF16)

    h = h_ref[...]
    u = jnp.dot(h, wu_ref[...], preferred_element_type=F32)
    gate = jnp.dot(h, wg_ref[...], preferred_element_type=F32)
    b1 = b1_ref[...]
    conv = cb_ref[...] + b0_ref[...] * cw_ref[0:1, :] + b1 * cw_ref[1:2, :] + gate * cw_ref[2:3, :]
    a_ref[...] = (conv * _sigmoid(conv) * u).astype(a_ref.dtype)
    n0_ref[...] = b1
    n1_ref[...] = gate


def ffn_up_step(x, g, w_ug, cw, cb, buf, tn):
    n_req, d = x.shape
    nt = D_FF // tn
    col = pl.BlockSpec((n_req, tn), lambda j: (0, j))
    return pl.pallas_call(
        _ffn_up_step_kernel,
        grid=(nt,),
        in_specs=[pl.BlockSpec((n_req, d), lambda j: (0, 0)),
                  pl.BlockSpec((1, d), lambda j: (0, 0)),
                  pl.BlockSpec((d, tn), lambda j: (0, j)),
                  pl.BlockSpec((d, tn), lambda j: (0, nt + j)),
                  pl.BlockSpec((3, tn), lambda j: (0, j)),
                  pl.BlockSpec((1, tn), lambda j: (0, j)),
                  col,
                  pl.BlockSpec((n_req, tn), lambda j: (0, nt + j))],
        out_specs=[col, col, col],
        out_shape=[jax.ShapeDtypeStruct((n_req, D_FF), BF16),
                   jax.ShapeDtypeStruct((n_req, D_FF), F32),
                   jax.ShapeDtypeStruct((n_req, D_FF), F32)],
        scratch_shapes=[pltpu.VMEM((n_req, d), BF16)],
        compiler_params=_params("arbitrary"),
        name="ffn_up_step",
    )(x, g, w_ug, w_ug, cw, cb, buf, buf)


def _pair_ones():
    r = lax.broadcasted_iota(jnp.int32, (LANES, LANES), 0) // RWKV_HEAD
    c = lax.broadcasted_iota(jnp.int32, (LANES, LANES), 1) // RWKV_HEAD
    return jnp.where(r == c, 1.0, 0.0).astype(BF16)


def _head_sum(x, ones, terms):
    total = None
    for _ in range(terms):
        piece = x.astype(BF16)
        part = jnp.dot(piece, ones, preferred_element_type=F32)
        total = part if total is None else total + part
        x = x - piece.astype(F32)
    return total


def _rwkv_proj_body(h, hprev, mu_ref, wrkv_ref, w1_ref, w2_ref, a1_ref, a2_ref, g1_ref, g2_ref, vec_ref,
                    vres, outs):
    r_ref, k_ref, v_ref, kk_ref, ka_ref, ld_ref, g_ref = outs
    dx = hprev - h
    mix = lambda i: (h + dx * mu_ref[i:i + 1, :]).astype(BF16)
    xr, xw, xk, xv, xa, xg = (mix(i) for i in range(6))
    dot = lambda a, b: jnp.dot(a, b, preferred_element_type=F32)
    w0, a0, k_k, k_a = (vec_ref[i:i + 1, :] for i in range(4))
    r = dot(xr, wrkv_ref[0])
    k = dot(xk, wrkv_ref[1])
    v = dot(xv, wrkv_ref[2])
    z = w0 + dot(jnp.tanh(dot(xw, w1_ref[...])).astype(BF16), w2_ref[...])
    logw = -(jnp.maximum(-z, 0.0) + jnp.log(1.0 + jnp.exp(-jnp.abs(z)))) - 0.5
    ld_ref[...] = -jnp.exp(logw)
    a = _sigmoid(a0 + dot(dot(xa, a1_ref[...]).astype(BF16), a2_ref[...]))
    g_ref[...] = dot(_sigmoid(dot(xg, g1_ref[...])).astype(BF16), g2_ref[...]).astype(g_ref.dtype)
    if vres is not None:
        vf_ref, v1_ref, v2_ref = vres
        v0 = vec_ref[4:5, :]
        v = v + (vf_ref[...] - v) * _sigmoid(v0 + dot(dot(xv, v1_ref[...]).astype(BF16), v2_ref[...]))
    kkr = k * k_k
    ones = _pair_ones()
    kk = jnp.concatenate(
        [kkr[:, t:t + LANES] * lax.rsqrt(jnp.maximum(
            _head_sum(kkr[:, t:t + LANES] * kkr[:, t:t + LANES], ones, 2), 1e-12))
         for t in range(0, D_MODEL, LANES)], axis=1)
    r_ref[...] = r.astype(r_ref.dtype)
    k_ref[...] = (k * (1.0 + (a - 1.0) * k_a)).astype(k_ref.dtype)
    v_ref[...] = v.astype(v_ref.dtype)
    kk_ref[...] = kk.astype(kk_ref.dtype)
    ka_ref[...] = (kk * a).astype(ka_ref.dtype)


def _rwkv_proj_seq_kernel(*refs, tiles_per_seq, has_vres):
    x_ref, gn_ref = refs[0], refs[1]
    weights = refs[2:11]
    n_in = 11 + (3 if has_vres else 0)
    vres = refs[11:14] if has_vres else None
    outs = refs[n_in:n_in + 7]
    shift_ref, carry_ref = refs[n_in + 7], refs[n_in + 8]
    m = pl.program_id(0)

    @pl.when(m % tiles_per_seq == 0)
    def _():
        carry_ref[...] = jnp.zeros_like(carry_ref)

    h = _rms(x_ref[...], gn_ref[...])
    tm = h.shape[0]
    rows = lax.broadcasted_iota(jnp.int32, (tm, 1), 0)
    hprev = jnp.where(rows == 0, carry_ref[0:1, :], pltpu.roll(h, 1, axis=0))
    _rwkv_proj_body(h, hprev, *weights, vres, outs)
    carry_ref[0:1, :] = h[tm - 1:tm, :]
    shift_ref[0] = h[tm - 1:tm, :]


def _rwkv_proj_step_kernel(*refs, has_vres):
    x_ref, gn_ref, prev_ref = refs[0], refs[1], refs[2]
    weights = refs[3:12]
    n_in = 12 + (3 if has_vres else 0)
    vres = refs[12:15] if has_vres else None
    outs = refs[n_in:n_in + 7]
    shift_ref = refs[n_in + 7]
    h = _rms(x_ref[...], gn_ref[...])
    _rwkv_proj_body(h, prev_ref[...], *weights, vres, outs)
    shift_ref[...] = h


def rwkv_proj(x, gn, prev, wts, vres, tm, n_seq, tiles_per_seq):
    m, d = x.shape
    full = lambda a: pl.BlockSpec(a.shape, lambda i, _n=a.ndim: (0,) * _n)
    rowblk = pl.BlockSpec((tm, d), lambda i: (i, 0))
    seq = prev is None
    ins = [x, gn] + ([] if seq else [prev]) + list(wts)
    specs = [rowblk, full(gn)] + ([] if seq else [rowblk]) + [full(a) for a in wts]
    if vres is not None:
        vf, v1, v2 = vres
        ins += [vf, v1, v2]
        specs += [rowblk, full(v1), full(v2)]
    out_shape = [jax.ShapeDtypeStruct((m, d), F32)] * 7
    out_specs = [rowblk] * 7
    if seq:
        out_shape.append(jax.ShapeDtypeStruct((n_seq, 1, d), F32))
        out_specs.append(pl.BlockSpec((1, 1, d), lambda i: (i // tiles_per_seq, 0, 0)))
        body = functools.partial(_rwkv_proj_seq_kernel, tiles_per_seq=tiles_per_seq, has_vres=vres is not None)
        scratch = [pltpu.VMEM((8, d), F32)]
    else:
        out_shape.append(jax.ShapeDtypeStruct((m, d), F32))
        out_specs.append(rowblk)
        body = functools.partial(_rwkv_proj_step_kernel, has_vres=vres is not None)
        scratch = []
    return pl.pallas_call(
        body,
        grid=(m // tm,),
        in_specs=specs,
        out_specs=out_specs,
        out_shape=out_shape,
        scratch_shapes=scratch,
        compiler_params=_params("arbitrary"),
        name="rwkv_proj_seq" if seq else "rwkv_proj_step",
    )(*ins)


def _wkv_epilogue(y, r, k, v, g, lnw, lnb, rk, ones):
    inv_n = 1.0 / RWKV_HEAD
    rows = y.shape[0]
    sums = _head_sum(jnp.concatenate([y, r * k * rk], axis=0), ones, 3)
    yc = y - sums[0:rows] * inv_n
    yn = yc * lax.rsqrt(_head_sum(yc * yc, ones, 3) * inv_n + RWKV_GN_EPS)
    return (yn * lnw + lnb + sums[rows:2 * rows] * v) * g


def _stack_heads(x):
    first = lax.broadcasted_iota(jnp.int32, (1, LANES), 1) < RWKV_HEAD
    return jnp.concatenate([jnp.where(first, x, 0.0), jnp.where(first, 0.0, x)], axis=0)


def _wkv_seq_kernel(*refs, pairs, has_acc):
    r_ref, k_ref, v_ref, kk_ref, ka_ref, ld_ref, g_ref, lnw_ref, lnb_ref, rk_ref = refs[:10]
    z_ref, s_out_ref, s_ref = refs[10 + has_acc:]
    c = pl.program_id(1)
    L = WKV_CHUNK
    R = 2 * L

    @pl.when(c == 0)
    def _():
        s_ref[...] = jnp.zeros_like(s_ref)

    dot = lambda a, b: jnp.dot(a.astype(BF16), b.astype(BF16), preferred_element_type=F32)
    dot_nt = lambda a, b: lax.dot_general(a.astype(BF16), b.astype(BF16), NT_DIMS, preferred_element_type=F32)
    dot_tn = lambda a, b: lax.dot_general(a.astype(BF16), b.astype(BF16), TN_DIMS, preferred_element_type=F32)
    ones = _pair_ones()
    ti = lax.broadcasted_iota(jnp.int32, (L, L), 0)
    tj = lax.broadcasted_iota(jnp.int32, (L, L), 1)
    tri = jnp.where(ti >= tj, 1.0, 0.0).astype(BF16)
    ri = lax.broadcasted_iota(jnp.int32, (R, R), 0)
    rj = lax.broadcasted_iota(jnp.int32, (R, R), 1)
    same = (ri // L) == (rj // L)
    lower = same & ((ri % L) > (rj % L))
    lower_eq = same & ((ri % L) >= (rj % L))
    eye = jnp.where(ri == rj, 1.0, 0.0)

    for first in range(0, pairs, WKV_GROUP):
        _wkv_group(range(first, first + WKV_GROUP), refs[:10], z_ref, s_out_ref, s_ref, c,
                   (dot, dot_nt, dot_tn, ones, tri, lower, lower_eq, eye))

    @pl.when(c == pl.num_programs(1) - 1)
    def _():
        for p in range(pairs):
            s_out_ref[0, 0, 2 * p] = s_ref[p, 0:RWKV_HEAD, 0:RWKV_HEAD]
            s_out_ref[0, 0, 2 * p + 1] = s_ref[p, RWKV_HEAD:LANES, RWKV_HEAD:LANES]


def _wkv_group(P, ins, z_ref, s_out_ref, s_ref, c, consts):
    r_ref, k_ref, v_ref, kk_ref, ka_ref, ld_ref, g_ref, lnw_ref, lnb_ref, rk_ref = ins
    dot, dot_nt, dot_tn, ones, tri, lower, lower_eq, eye = consts
    L = WKV_CHUNK
    R = 2 * L
    P = list(P)
    sls = {p: slice(p * LANES, (p + 1) * LANES) for p in P}
    ld = {p: ld_ref[:, sls[p]] for p in P}
    ld_hi = {p: ld[p].astype(BF16) for p in P}
    ld_mid = {p: (ld[p] - ld_hi[p].astype(F32)).astype(BF16) for p in P}
    ld_lo = {p: (ld[p] - ld_hi[p].astype(F32) - ld_mid[p].astype(F32)).astype(BF16) for p in P}
    cum = {p: dot(tri, ld_hi[p]) + dot(tri, ld_mid[p]) + dot(tri, ld_lo[p]) for p in P}
    tot = {p: cum[p][L - 1:L, :] for p in P}
    ar, bk, bk_end, v_s = {}, {}, {}, {}
    for p in P:
        r, k, v, kk, ka = (ref[:, sls[p]] for ref in (r_ref, k_ref, v_ref, kk_ref, ka_ref))
        dec_out = jnp.exp(-cum[p])
        dec_end = jnp.exp(tot[p] - cum[p])
        ar[p] = jnp.concatenate([_stack_heads(jnp.exp(cum[p] - ld[p]) * (-kk)),
                                 _stack_heads(jnp.exp(cum[p]) * r)], axis=0).astype(BF16)
        bk[p] = jnp.concatenate([_stack_heads(dec_out * ka), _stack_heads(dec_out * k)], axis=0).astype(BF16)
        bk_end[p] = jnp.concatenate([_stack_heads(dec_end * ka), _stack_heads(dec_end * k)], axis=0).astype(BF16)
        v_s[p] = _stack_heads(v)
    big = {p: dot_nt(ar[p], bk[p]) for p in P}
    n_ab = {p: jnp.where(lower, big[p][0:R, 0:R], 0.0) for p in P}
    a_ak = {p: jnp.where(lower, big[p][0:R, R:2 * R], 0.0).astype(BF16) for p in P}
    a_r = {p: jnp.concatenate([jnp.where(lower_eq, big[p][R:2 * R, 0:R], 0.0),
                               jnp.where(lower_eq, big[p][R:2 * R, R:2 * R], 0.0)], axis=1).astype(BF16)
           for p in P}
    inv = {p: eye + n_ab[p] for p in P}
    pw = n_ab
    for _ in range(int(math.log2(L)) - 1):
        pw = {p: dot(pw[p], pw[p]) for p in P}
        inv = {p: inv[p] + dot(inv[p], pw[p]) for p in P}
    s_old = {p: s_ref[p] for p in P}
    ars = {p: dot_nt(ar[p], s_old[p]) for p in P}
    akv = {p: dot(a_ak[p], v_s[p]) for p in P}
    u_s = {p: dot(inv[p], ars[p][0:R] + akv[p]) for p in P}
    uv = {p: jnp.concatenate([u_s[p], v_s[p]], axis=0).astype(BF16) for p in P}
    y_s = {p: ars[p][R:2 * R] + dot(a_r[p], uv[p]) for p in P}
    s_new = {p: s_old[p] * jnp.exp(tot[p]) + dot_tn(uv[p], bk_end[p]) for p in P}
    for p in P:
        s_ref[p] = s_new[p]
    for p in P:
        sl = sls[p]
        y = y_s[p][0:L, :] + y_s[p][L:R, :]
        z_ref[:, sl] = _wkv_epilogue(y, r_ref[:, sl], k_ref[:, sl], v_ref[:, sl], g_ref[:, sl], lnw_ref[:, sl],
                                     lnb_ref[:, sl], rk_ref[:, sl], ones).astype(z_ref.dtype)


def wkv_seq(r, k, v, kk, ka, ld, g, lnw, lnb, rk, n_seq, chunks, layer, n_layers, acc):
    m, d = r.shape
    L = WKV_CHUNK
    blk = pl.BlockSpec((L, d), lambda b, c: (b * chunks + c, 0))
    vec = pl.BlockSpec((1, d), lambda b, c: (0, 0))
    ins = [r, k, v, kk, ka, ld, g, lnw, lnb, rk]
    specs = [blk] * 7 + [vec] * 3
    aliases = _layer_slab(acc, ins, specs, 1)
    return pl.pallas_call(
        functools.partial(_wkv_seq_kernel, pairs=d // LANES, has_acc=acc is not None),
        grid=(n_seq, chunks),
        in_specs=specs,
        out_specs=[blk, pl.BlockSpec((1, 1, RWKV_HEADS, RWKV_HEAD, RWKV_HEAD), lambda b, c: (layer, b, 0, 0, 0))],
        out_shape=[jax.ShapeDtypeStruct((m, d), BF16),
                   jax.ShapeDtypeStruct((n_layers, n_seq, RWKV_HEADS, RWKV_HEAD, RWKV_HEAD), F32)],
        scratch_shapes=[pltpu.VMEM((d // LANES, LANES, LANES), F32)],
        input_output_aliases=aliases,
        compiler_params=_params("parallel", "arbitrary"),
        name="wkv_seq",
    )(*ins)


def _wkv_step_kernel(*refs, has_acc):
    r_ref, k_ref, v_ref, kk_ref, ka_ref, ld_ref, g_ref, lnw_ref, lnb_ref, rk_ref, s_ref = refs[:11]
    z_ref, s_out_ref = refs[11 + has_acc:]
    N = RWKV_HEAD
    eye = lax.broadcasted_iota(jnp.int32, (1, N, N), 1) == lax.broadcasted_iota(jnp.int32, (1, N, N), 2)
    lnw, lnb, rk = lnw_ref[...], lnb_ref[...], rk_ref[...]

    def one_request(b, carry):
        r, k, v, kk, ka, ld, g = (ref[b] for ref in (r_ref, k_ref, v_ref, kk_ref, ka_ref, ld_ref, g_ref))
        s_old = s_ref[0, b]
        sa = jnp.sum(s_old * (-kk), axis=-1, keepdims=True)
        v_col = jnp.sum(jnp.where(eye, v, 0.0), axis=-1, keepdims=True)
        s_new = s_old * jnp.exp(ld) + sa * ka + v_col * k
        s_out_ref[0, b] = s_new
        y_col = jnp.sum(s_new * r, axis=-1, keepdims=True)
        y = jnp.sum(jnp.where(eye, y_col, 0.0), axis=1, keepdims=True)
        yc = y - jnp.mean(y, axis=-1, keepdims=True)
        yn = yc * lax.rsqrt(jnp.mean(yc * yc, axis=-1, keepdims=True) + RWKV_GN_EPS)
        bonus = jnp.sum(r * k * rk, axis=-1, keepdims=True) * v
        z_ref[b] = (yn * lnw + lnb + bonus) * g
        return carry

    lax.fori_loop(0, z_ref.shape[0], one_request, 0)


def wkv_step(r, k, v, kk, ka, ld, g, lnw, lnb, rk, states, layer, acc, requests_per_step=8):
    n = r.shape[0]
    bb = requests_per_step
    row = pl.BlockSpec((bb,) + r.shape[1:], lambda i: (i, 0, 0, 0))
    vec = pl.BlockSpec(lnw.shape, lambda i: (0, 0, 0))
    st = pl.BlockSpec((1, bb) + states.shape[2:], lambda i: (layer, i, 0, 0, 0))
    ins = [r, k, v, kk, ka, ld, g, lnw, lnb, rk, states]
    specs = [row] * 7 + [vec] * 3 + [st]
    aliases = _layer_slab(acc, ins, specs, 1)
    return pl.pallas_call(
        functools.partial(_wkv_step_kernel, has_acc=acc is not None),
        grid=(n // bb,),
        in_specs=specs,
        out_specs=[row, st],
        out_shape=[jax.ShapeDtypeStruct(r.shape, F32), jax.ShapeDtypeStruct(states.shape, F32)],
        input_output_aliases=aliases,
        compiler_params=_params("parallel"),
        name="wkv_step",
    )(*ins)


def _trunk(x, seq, states, p):
    depth = p['norm_mix'].shape[0]
    rows = x.shape[0]
    if seq is not None:
        n_seq, tp, lead = seq
        tm = _row_tile(tp, 1088)
        tm_small = _row_tile(tp, 544)
        tm_rwkv = _row_tile(tp, 272)
        tiles_small, tiles_rwkv = tp // tm_small, tp // tm_rwkv
        cos, sin = rot_table(tp, -lead, 1)
    else:
        ret_s, wkv_s, shift_s, conv_s = states
        tm = tm_small = rows
        tiles_small = 1
        lead = 0
        cos, sin = rot_table(8, PAST_LEN, 0)
    new_ret = new_wkv = None
    new_shift, new_conv = [], []
    n_ret, n_rwkv = (depth + 1) // 2, depth // 2
    v_first = None
    for i in range(depth):
        j = i // 2
        gn = p['norm_mix'][i][None]
        if i % 2 == 0:
            gnw = p['ret_gn_w'][j][None]
            if seq is not None:
                proj = norm_proj(x, gn, p['ret_w_in'][j], tm, 1024, BF16)
                o, new_ret = retention_prompt(proj, cos, sin, gnw, n_seq, tp // RET_CHUNK, j, n_ret, new_ret)
            else:
                proj = norm_proj(x, gn, p['ret_w_in'][j], tm, 1024, F32)
                o, new_ret = retention_step(proj[:, None, :], cos, sin, gnw, ret_s, j, new_ret)
                o = o[:, 0, :]
            x = matmul_res(o, p['ret_w_out'][j], x, tm_small, tiles_small, lead)
        else:
            vecs = [p['rwkv_w0'][j], p['rwkv_a0'][j], p['rwkv_k_k'][j], p['rwkv_k_a'][j]]
            vecs.append(p['rwkv_v0'][j - 1] if j else jnp.zeros_like(vecs[0]))
            vecs = jnp.stack(vecs + [jnp.zeros_like(vecs[0])] * 3)
            wts = [p['rwkv_mu'][j], p['rwkv_w_rkv'][j], p['rwkv_w1'][j], p['rwkv_w2'][j], p['rwkv_a1'][j],
                   p['rwkv_a2'][j], p['rwkv_g1'][j], p['rwkv_g2'][j], vecs]
            vres = (v_first, p['rwkv_v1'][j - 1], p['rwkv_v2'][j - 1]) if j else None
            lnw, lnb = p['rwkv_ln_w'][j][None], p['rwkv_ln_b'][j][None]
            rk = p['rwkv_r_k'][j].reshape(1, D_MODEL)
            if seq is not None:
                r, k, v, kk, ka, ld, g, sh = rwkv_proj(x, gn, None, wts, vres, tm_rwkv, n_seq, tiles_rwkv)
                z, new_wkv = wkv_seq(r, k, v, kk, ka, ld, g, lnw, lnb, rk, n_seq, tp // WKV_CHUNK,
                                     j, n_rwkv, new_wkv)
                sh = sh[:, 0, :]
            else:
                r, k, v, kk, ka, ld, g, sh = rwkv_proj(x, gn, shift_s[j], wts, vres, tm, 1, 1)
                heads = lambda t: t.reshape(-1, RWKV_HEADS, 1, RWKV_HEAD)
                z, new_wkv = wkv_step(*(heads(t) for t in (r, k, v, kk, ka, ld, g)),
                                      *(heads(t)[0] for t in (lnw, lnb, rk)), wkv_s, j, new_wkv)
                z = z.reshape(rows, D_MODEL)
            if v_first is None:
                v_first = v
            new_shift.append(sh)
            x = matmul_res(z, p['rwkv_w_o'][j], x, tm_small, tiles_small, lead)
        gf = p['norm_ffn'][i][None]
        cw, cb = p['ffn_conv_w'][i], p['ffn_conv_b'][i][None]
        if seq is not None:
            a, cbuf = ffn_up_seq(x, gf, p['ffn_w_ug'][i], cw, cb, tm_small, n_seq, tiles_small)
        else:
            a, n0, n1 = ffn_up_step(x, gf, p['ffn_w_ug'][i], cw, cb, conv_s[i].reshape(rows, 2 * D_FF), D_FF // 2)
            cbuf = jnp.stack([n0, n1], axis=1)
        new_conv.append(cbuf)
        x = matmul_res(a, p['ffn_w_d'][i], x, tm_small, tiles_small, lead)
    return x, new_ret, new_wkv, jnp.stack(new_shift), jnp.stack(new_conv)


def kernel(x_prompt, x_sample, state_ret, state_wkv, state_shift, state_conv, meta_tokens, norm_mix, norm_ffn, norm_final, ret_w_in, ret_gn_w, ret_w_out, rwkv_mu, rwkv_w_rkv, rwkv_w0, rwkv_w1, rwkv_w2, rwkv_a0, rwkv_a1, rwkv_a2, rwkv_v0, rwkv_v1, rwkv_v2, rwkv_g1, rwkv_g2, rwkv_k_k, rwkv_k_a, rwkv_r_k, rwkv_ln_w, rwkv_ln_b, rwkv_w_o, ffn_w_ug, ffn_conv_w, ffn_conv_b, ffn_w_d):
    bf = lambda w: w.astype(BF16)
    p = dict(norm_mix=norm_mix, norm_ffn=norm_ffn, ret_w_in=bf(ret_w_in), ret_gn_w=ret_gn_w,
             ret_w_out=bf(ret_w_out), rwkv_mu=rwkv_mu, rwkv_w_rkv=bf(rwkv_w_rkv), rwkv_w0=rwkv_w0,
             rwkv_w1=bf(rwkv_w1), rwkv_w2=bf(rwkv_w2), rwkv_a0=rwkv_a0, rwkv_a1=bf(rwkv_a1),
             rwkv_a2=bf(rwkv_a2), rwkv_v0=rwkv_v0, rwkv_v1=bf(rwkv_v1), rwkv_v2=bf(rwkv_v2),
             rwkv_g1=bf(rwkv_g1), rwkv_g2=bf(rwkv_g2), rwkv_k_k=rwkv_k_k, rwkv_k_a=rwkv_k_a,
             rwkv_r_k=rwkv_r_k, rwkv_ln_w=rwkv_ln_w, rwkv_ln_b=rwkv_ln_b, rwkv_w_o=bf(rwkv_w_o),
             ffn_w_ug=bf(ffn_w_ug), ffn_conv_w=ffn_conv_w, ffn_conv_b=ffn_conv_b, ffn_w_d=bf(ffn_w_d))
    B, S, D = x_prompt.shape
    lead = (-N_META) % RET_CHUNK
    tp = lead + N_META + S
    assert tp % RET_CHUNK == 0 and D == D_MODEL
    meta = jnp.broadcast_to(meta_tokens[None].astype(F32), (B, N_META, D))
    xp = jnp.concatenate([jnp.zeros((B, lead, D), F32), meta, x_prompt.astype(F32)], axis=1).reshape(B * tp, D)
    xp, p_ret, p_wkv, p_shift, p_conv = _trunk(xp, (B, tp, lead), None, p)
    skip = (lead + N_META) // RET_CHUNK
    y_prompt = final_norm(xp, norm_final[None], B, tp // RET_CHUNK, skip, RET_CHUNK).reshape(B, S, D)

    n_req = x_sample.shape[0]
    xs, s_ret, s_wkv, s_shift, s_conv = _trunk(
        x_sample.reshape(n_req, D).astype(F32), None, (state_ret, state_wkv, state_shift, state_conv), p)
    y_sample = final_norm(xs, norm_final[None], 1, 1, 0, n_req).reshape(n_req, 1, D)
    return (y_prompt.astype(x_prompt.dtype), y_sample.astype(x_sample.dtype),
            p_ret, p_wkv, p_shift, p_conv, s_ret, s_wkv, s_shift, s_conv)
```

```python
import functools
import math

import jax
import jax.numpy as jnp
from jax import lax
from jax.experimental import pallas as pl
from jax.experimental.pallas import tpu as pltpu

F32 = jnp.float32
BF16 = jnp.bfloat16

D_MODEL = 1024
N_META = 16
PAST_LEN = 16384
RET_HEADS = 4
RET_DK = D_MODEL // RET_HEADS
RET_DV = 2 * RET_DK
RET_VDIM = RET_HEADS * RET_DV
RET_CHUNK = 128
RWKV_HEAD = 64
RWKV_HEADS = D_MODEL // RWKV_HEAD
D_FF = 2816
RMS_EPS = 1e-6
RET_GN_EPS = 1e-5
RWKV_GN_EPS = 64e-5

LANES = 128
WKV_CHUNK = 64
WKV_GROUP = 8
VMEM_LIMIT = 56 * 1024 * 1024

NT_DIMS = (((1,), (1,)), ((), ()))
TN_DIMS = (((0,), (0,)), ((), ()))


def _params(*sem):
    return pltpu.CompilerParams(dimension_semantics=sem, vmem_limit_bytes=VMEM_LIMIT)


def _rms(x, g):
    return x * lax.rsqrt(jnp.mean(x * x, axis=-1, keepdims=True) + RMS_EPS) * g


def _sigmoid(x):
    return 1.0 / (1.0 + jnp.exp(-x))


def _row_tile(rows_per_seq, cap):
    best = None
    for t in range(16, min(rows_per_seq, cap) + 1, 16):
        if rows_per_seq % t == 0:
            best = t
    assert best is not None, rows_per_seq
    return best


def _rot_table_kernel(inv_ref, cos_ref, sin_ref, *, pos0, step):
    rows = lax.broadcasted_iota(jnp.int32, cos_ref.shape, 0)
    pos = (rows * step + pos0).astype(F32)
    ang = pos * inv_ref[...]
    cos_ref[...] = jnp.cos(ang)
    sin_ref[...] = jnp.sin(ang)


def rot_table(n_rows, pos0, step):
    half = RET_DK // 2
    inv = (1.0 / (10000.0 ** jnp.linspace(0.0, 1.0, half, dtype=F32))).reshape(1, half)
    return pl.pallas_call(
        functools.partial(_rot_table_kernel, pos0=pos0, step=step),
        out_shape=(jax.ShapeDtypeStruct((n_rows, half), F32),) * 2,
        name="rot_table",
    )(inv)


def _norm_proj_kernel(x_ref, g_ref, w_ref, o_ref, h_ref):
    @pl.when(pl.program_id(1) == 0)
    def _():
        h_ref[...] = _rms(x_ref[...], g_ref[...]).astype(BF16)

    o_ref[...] = jnp.dot(h_ref[...], w_ref[...], preferred_element_type=F32).astype(o_ref.dtype)


def norm_proj(x, g, w, tm, tn, out_dtype):
    m, d = x.shape
    n = w.shape[1]
    return pl.pallas_call(
        _norm_proj_kernel,
        grid=(m // tm, n // tn),
        in_specs=[pl.BlockSpec((tm, d), lambda i, j: (i, 0)),
                  pl.BlockSpec((1, d), lambda i, j: (0, 0)),
                  pl.BlockSpec((d, tn), lambda i, j: (0, j))],
        out_specs=pl.BlockSpec((tm, tn), lambda i, j: (i, j)),
        out_shape=jax.ShapeDtypeStruct((m, n), out_dtype),
        scratch_shapes=[pltpu.VMEM((tm, d), BF16)],
        compiler_params=_params("parallel", "arbitrary"),
        name="norm_proj",
    )(x, g, w)


def _matmul_res_kernel(a_ref, w_ref, x_ref, o_ref, *, tiles_per_seq, lead):
    y = x_ref[...] + jnp.dot(a_ref[...].astype(BF16), w_ref[...], preferred_element_type=F32)
    if lead:
        tm = y.shape[0]
        row = (pl.program_id(0) % tiles_per_seq) * tm + lax.broadcasted_iota(jnp.int32, (tm, 1), 0)
        y = jnp.where(row >= lead, y, 0.0)
    o_ref[...] = y


def matmul_res(a, w, x, tm, tiles_per_seq, lead):
    m, k = a.shape
    n = w.shape[1]
    return pl.pallas_call(
        functools.partial(_matmul_res_kernel, tiles_per_seq=tiles_per_seq, lead=lead),
        grid=(m // tm,),
        in_specs=[pl.BlockSpec((tm, k), lambda i: (i, 0)),
                  pl.BlockSpec((k, n), lambda i: (0, 0)),
                  pl.BlockSpec((tm, n), lambda i: (i, 0))],
        out_specs=pl.BlockSpec((tm, n), lambda i: (i, 0)),
        out_shape=jax.ShapeDtypeStruct((m, n), F32),
        compiler_params=_params("parallel"),
        name="matmul_res",
    )(a, w, x)


def _final_norm_kernel(x_ref, g_ref, o_ref):
    o_ref[...] = _rms(x_ref[...], g_ref[...])


def final_norm(x, g, n_seq, chunks_per_seq, skip_chunks, rows):
    d = x.shape[1]
    keep = chunks_per_seq - skip_chunks
    return pl.pallas_call(
        _final_norm_kernel,
        grid=(n_seq, keep),
        in_specs=[pl.BlockSpec((rows, d), lambda b, c: (b * chunks_per_seq + skip_chunks + c, 0)),
                  pl.BlockSpec((1, d), lambda b, c: (0, 0))],
        out_specs=pl.BlockSpec((rows, d), lambda b, c: (b * keep + c, 0)),
        out_shape=jax.ShapeDtypeStruct((n_seq * keep * rows, d), F32),
        compiler_params=_params("parallel", "parallel"),
        name="final_norm",
    )(x, g)


def _rotate(x, cos, sin):
    half = RET_DK // 2
    x1, x2 = x[:, :half], x[:, half:]
    return jnp.concatenate([x1 * cos - x2 * sin, x1 * sin + x2 * cos], axis=1)


def _gated_head_norm(o, g, gnw):
    oc = o - jnp.mean(o, axis=-1, keepdims=True)
    on = oc * lax.rsqrt(jnp.mean(oc * oc, axis=-1, keepdims=True) + RET_GN_EPS)
    return (g * _sigmoid(g)) * (on * gnw)


def _log_g(h):
    return math.log(1.0 - 2.0 ** (-5.0 - h))


def _layer_slab(acc, ins, specs, out_index):
    if acc is None:
        return {}
    ins.append(acc)
    specs.append(pl.BlockSpec(memory_space=pl.ANY))
    return {len(ins) - 1: out_index}


def _retention_kernel(*refs, has_acc):
    p_ref, cos_ref, sin_ref, gnw_ref = refs[:4]
    o_ref, s_out_ref, s_ref = refs[4 + has_acc:]
    c = pl.program_id(1)
    L = RET_CHUNK
    heads = range(RET_HEADS)
    qk = RET_HEADS * RET_DK

    @pl.when(c == 0)
    def _():
        s_ref[...] = jnp.zeros_like(s_ref)

    dot = lambda a, b: jnp.dot(a, b, preferred_element_type=F32)
    cos, sin = cos_ref[...], sin_ref[...]
    ri = lax.broadcasted_iota(jnp.int32, (L, 1), 0).astype(F32)
    ci = lax.broadcasted_iota(jnp.int32, (1, L), 1).astype(F32)
    rel = ri - ci
    lg = [_log_g(h) for h in heads]
    q = [_rotate(p_ref[:, h * RET_DK:(h + 1) * RET_DK].astype(F32), cos, sin).astype(BF16) for h in heads]
    kf = [_rotate(p_ref[:, qk + h * RET_DK:qk + (h + 1) * RET_DK].astype(F32), cos, sin) * (RET_DK ** -0.5)
          for h in heads]
    v = [p_ref[:, 2 * qk + h * RET_DV:2 * qk + (h + 1) * RET_DV] for h in heads]
    s_old = [s_ref[h] for h in heads]
    scores = [lax.dot_general(q[h], kf[h].astype(BF16), NT_DIMS, preferred_element_type=F32)
              * jnp.where(rel >= 0, jnp.exp(lg[h] * jnp.maximum(rel, 0.0)), 0.0) for h in heads]
    qs = [dot(q[h], s_old[h].astype(BF16)) for h in heads]
    o = [dot(scores[h].astype(BF16), v[h]) + jnp.exp(lg[h] * (ri + 1.0)) * qs[h] for h in heads]
    kd = [(kf[h] * jnp.exp(lg[h] * (L - 1.0 - ri))).astype(BF16) for h in heads]
    s_new = [math.exp(lg[h] * L) * s_old[h] + lax.dot_general(kd[h], v[h], TN_DIMS, preferred_element_type=F32)
             for h in heads]
    for h in heads:
        s_ref[h] = s_new[h]
    g0 = 2 * qk + RET_VDIM
    for h in heads:
        sl = slice(h * RET_DV, (h + 1) * RET_DV)
        g = p_ref[:, g0 + h * RET_DV:g0 + (h + 1) * RET_DV].astype(F32)
        o_ref[:, sl] = _gated_head_norm(o[h], g, gnw_ref[:, sl]).astype(o_ref.dtype)

    @pl.when(c == pl.num_programs(1) - 1)
    def _():
        for h in heads:
            s_out_ref[0, 0, h] = s_new[h]


def retention_prompt(proj, cos, sin, gnw, n_seq, chunks, layer, n_layers, acc):
    L = RET_CHUNK
    m, width = proj.shape
    ins = [proj, cos, sin, gnw]
    specs = [pl.BlockSpec((L, width), lambda b, c: (b * chunks + c, 0)),
             pl.BlockSpec((L, RET_DK // 2), lambda b, c: (c, 0)),
             pl.BlockSpec((L, RET_DK // 2), lambda b, c: (c, 0)),
             pl.BlockSpec((1, RET_VDIM), lambda b, c: (0, 0))]
    aliases = _layer_slab(acc, ins, specs, 1)
    return pl.pallas_call(
        functools.partial(_retention_kernel, has_acc=acc is not None),
        grid=(n_seq, chunks),
        in_specs=specs,
        out_specs=[pl.BlockSpec((L, RET_VDIM), lambda b, c: (b * chunks + c, 0)),
                   pl.BlockSpec((1, 1, RET_HEADS, RET_DK, RET_DV), lambda b, c: (layer, b, 0, 0, 0))],
        out_shape=[jax.ShapeDtypeStruct((m, RET_VDIM), BF16),
                   jax.ShapeDtypeStruct((n_layers, n_seq, RET_HEADS, RET_DK, RET_DV), F32)],
        scratch_shapes=[pltpu.VMEM((RET_HEADS, RET_DK, RET_DV), F32)],
        input_output_aliases=aliases,
        compiler_params=_params("parallel", "arbitrary"),
        name="retention_prompt",
    )(*ins)


def _to_column(row, n):
    eye = lax.broadcasted_iota(jnp.int32, (n, n), 0) == lax.broadcasted_iota(jnp.int32, (n, n), 1)
    return jnp.sum(jnp.where(eye, jnp.broadcast_to(row, (n, n)), 0.0), axis=-1, keepdims=True)


def _retention_step_kernel(*refs, has_acc):
    p_ref, cos_ref, sin_ref, gnw_ref, s_ref = refs[:5]
    o_ref, s_out_ref = refs[5 + has_acc:]
    cos, sin = cos_ref[0:1, :], sin_ref[0:1, :]
    qk = RET_HEADS * RET_DK
    for h in range(RET_HEADS):
        decay = math.exp(_log_g(h))
        q = _rotate(p_ref[0, :, h * RET_DK:(h + 1) * RET_DK], cos, sin)
        k = _rotate(p_ref[0, :, qk + h * RET_DK:qk + (h + 1) * RET_DK], cos, sin) * (RET_DK ** -0.5)
        v = p_ref[0, :, 2 * qk + h * RET_DV:2 * qk + (h + 1) * RET_DV]
        g = p_ref[0, :, 2 * qk + RET_VDIM + h * RET_DV:2 * qk + RET_VDIM + (h + 1) * RET_DV]
        s_old = s_ref[0, 0, h]
        qs = jnp.sum(s_old * _to_column(q, RET_DK), axis=0, keepdims=True)
        o = jnp.sum(q * k, axis=-1, keepdims=True) * v + decay * qs
        s_out_ref[0, 0, h] = decay * s_old + _to_column(k, RET_DK) * v
        o_ref[0, :, h * RET_DV:(h + 1) * RET_DV] = _gated_head_norm(
            o, g, gnw_ref[:, h * RET_DV:(h + 1) * RET_DV]).astype(o_ref.dtype)


def retention_step(proj, cos, sin, gnw, states, layer, acc):
    n = proj.shape[0]
    st = pl.BlockSpec((1, 1, RET_HEADS, RET_DK, RET_DV), lambda b: (layer, b, 0, 0, 0))
    ins = [proj, cos, sin, gnw, states]
    specs = [pl.BlockSpec((1, 1, proj.shape[2]), lambda b: (b, 0, 0)),
             pl.BlockSpec(cos.shape, lambda b: (0, 0)),
             pl.BlockSpec(sin.shape, lambda b: (0, 0)),
             pl.BlockSpec((1, RET_VDIM), lambda b: (0, 0)),
             st]
    aliases = _layer_slab(acc, ins, specs, 1)
    return pl.pallas_call(
        functools.partial(_retention_step_kernel, has_acc=acc is not None),
        grid=(n,),
        in_specs=specs,
        out_specs=[pl.BlockSpec((1, 1, RET_VDIM), lambda b: (b, 0, 0)), st],
        out_shape=[jax.ShapeDtypeStruct((n, 1, RET_VDIM), BF16),
                   jax.ShapeDtypeStruct(states.shape, F32)],
        input_output_aliases=aliases,
        compiler_params=_params("parallel"),
        name="retention_step",
    )(*ins)


FFN_COLS = 256


def _conv_gate(gate, p1, p2, cw, cb):
    rows = lax.broadcasted_iota(jnp.int32, (gate.shape[0], 1), 0)
    g1 = jnp.where(rows == 0, p1, pltpu.roll(gate, 1, axis=0))
    g2 = jnp.where(rows == 0, p2, jnp.where(rows == 1, p1, pltpu.roll(gate, 2, axis=0)))
    return cb + g2 * cw[0:1, :] + g1 * cw[1:2, :] + gate * cw[2:3, :]


def _ffn_up_seq_kernel(x_ref, g_ref, w_ref, cw_ref, cb_ref, a_ref, nc_ref, carry_ref, *, tiles_per_seq):
    @pl.when(pl.program_id(0) % tiles_per_seq == 0)
    def _():
        carry_ref[...] = jnp.zeros_like(carry_ref)

    h = _rms(x_ref[...], g_ref[...]).astype(BF16)
    tm = h.shape[0]
    for j in range(D_FF // FFN_COLS):
        sl = slice(j * FFN_COLS, (j + 1) * FFN_COLS)
        u = jnp.dot(h, w_ref[:, sl], preferred_element_type=F32)
        gate = jnp.dot(h, w_ref[:, D_FF + j * FFN_COLS:D_FF + (j + 1) * FFN_COLS], preferred_element_type=F32)
        conv = _conv_gate(gate, carry_ref[1:2, sl], carry_ref[0:1, sl], cw_ref[:, sl], cb_ref[:, sl])
        a_ref[:, sl] = (conv * _sigmoid(conv) * u).astype(a_ref.dtype)
        last = gate[tm - 2:tm, :]
        carry_ref[0:2, sl] = last
        nc_ref[0, :, sl] = last


def ffn_up_seq(x, g, w_ug, cw, cb, tm, n_seq, tiles_per_seq):
    m, d = x.shape
    return pl.pallas_call(
        functools.partial(_ffn_up_seq_kernel, tiles_per_seq=tiles_per_seq),
        grid=(m // tm,),
        in_specs=[pl.BlockSpec((tm, d), lambda i: (i, 0)),
                  pl.BlockSpec((1, d), lambda i: (0, 0)),
                  pl.BlockSpec(w_ug.shape, lambda i: (0, 0)),
                  pl.BlockSpec(cw.shape, lambda i: (0, 0)),
                  pl.BlockSpec(cb.shape, lambda i: (0, 0))],
        out_specs=[pl.BlockSpec((tm, D_FF), lambda i: (i, 0)),
                   pl.BlockSpec((1, 2, D_FF), lambda i: (i // tiles_per_seq, 0, 0))],
        out_shape=[jax.ShapeDtypeStruct((m, D_FF), BF16),
                   jax.ShapeDtypeStruct((n_seq, 2, D_FF), F32)],
        scratch_shapes=[pltpu.VMEM((8, D_FF), F32)],
        compiler_params=_params("arbitrary"),
        name="ffn_up_seq",
    )(x, g, w_ug, cw, cb)


def _ffn_up_step_kernel(x_ref, g_ref, wu_ref, wg_ref, cw_ref, cb_ref, b0_ref, b1_ref,
                        a_ref, n0_ref, n1_ref, h_ref):
    @pl.when(pl.program_id(0) == 0)
    def _():
        h_ref[...] = _rms(x_ref[...], g_ref[...]).astype(BF16)

    h = h_ref[...]
    u = jnp.dot(h, wu_ref[...], preferred_element_type=F32)
    gate = jnp.dot(h, wg_ref[...], preferred_element_type=F32)
    b1 = b1_ref[...]
    conv = cb_ref[...] + b0_ref[...] * cw_ref[0:1, :] + b1 * cw_ref[1:2, :] + gate * cw_ref[2:3, :]
    a_ref[...] = (conv * _sigmoid(conv) * u).astype(a_ref.dtype)
    n0_ref[...] = b1
    n1_ref[...] = gate


def ffn_up_step(x, g, w_ug, cw, cb, buf, tn):
    n_req, d = x.shape
    nt = D_FF // tn
    col = pl.BlockSpec((n_req, tn), lambda j: (0, j))
    return pl.pallas_call(
        _ffn_up_step_kernel,
        grid=(nt,),
        in_specs=[pl.BlockSpec((n_req, d), lambda j: (0, 0)),
                  pl.BlockSpec((1, d), lambda j: (0, 0)),
                  pl.BlockSpec((d, tn), lambda j: (0, j)),
                  pl.BlockSpec((d, tn), lambda j: (0, nt + j)),
                  pl.BlockSpec((3, tn), lambda j: (0, j)),
                  pl.BlockSpec((1, tn), lambda j: (0, j)),
                  col,
                  pl.BlockSpec((n_req, tn), lambda j: (0, nt + j))],
        out_specs=[col, col, col],
        out_shape=[jax.ShapeDtypeStruct((n_req, D_FF), BF16),
                   jax.ShapeDtypeStruct((n_req, D_FF), F32),
                   jax.ShapeDtypeStruct((n_req, D_FF), F32)],
        scratch_shapes=[pltpu.VMEM((n_req, d), BF16)],
        compiler_params=_params("arbitrary"),
        name="ffn_up_step",
    )(x, g, w_ug, w_ug, cw, cb, buf, buf)


def _pair_ones():
    r = lax.broadcasted_iota(jnp.int32, (LANES, LANES), 0) // RWKV_HEAD
    c = lax.broadcasted_iota(jnp.int32, (LANES, LANES), 1) // RWKV_HEAD
    return jnp.where(r == c, 1.0, 0.0).astype(BF16)


def _head_sum(x, ones, terms):
    total = None
    for _ in range(terms):
        piece = x.astype(BF16)
        part = jnp.dot(piece, ones, preferred_element_type=F32)
        total = part if total is None else total + part
        x = x - piece.astype(F32)
    return total


def _rwkv_proj_body(h, hprev, mu_ref, wrkv_ref, w1_ref, w2_ref, a1_ref, a2_ref, g1_ref, g2_ref, vec_ref,
                    vres, outs, mix_dtype):
    r_ref, k_ref, v_ref, kk_ref, ka_ref, ld_ref, g_ref = outs
    hm, dxm = h.astype(mix_dtype), (hprev - h).astype(mix_dtype)
    mix = lambda i: (hm + dxm * mu_ref[i:i + 1, :].astype(mix_dtype)).astype(BF16)
    xr, xw, xk, xv, xa, xg = (mix(i) for i in range(6))
    dot = lambda a, b: jnp.dot(a, b, preferred_element_type=F32)
    w0, a0, k_k, k_a = (vec_ref[i:i + 1, :] for i in range(4))
    r = dot(xr, wrkv_ref[0])
    k = dot(xk, wrkv_ref[1])
    v = dot(xv, wrkv_ref[2])
    z = w0 + dot(jnp.tanh(dot(xw, w1_ref[...])).astype(BF16), w2_ref[...])
    logw = -(jnp.maximum(-z, 0.0) + jnp.log(1.0 + jnp.exp(-jnp.abs(z)))) - 0.5
    ld_ref[...] = -jnp.exp(logw)
    a = _sigmoid(a0 + dot(dot(xa, a1_ref[...]).astype(BF16), a2_ref[...]))
    g_ref[...] = dot(_sigmoid(dot(xg, g1_ref[...])).astype(BF16), g2_ref[...]).astype(g_ref.dtype)
    if vres is not None:
        vf_ref, v1_ref, v2_ref = vres
        v0 = vec_ref[4:5, :]
        v = v + (vf_ref[...] - v) * _sigmoid(v0 + dot(dot(xv, v1_ref[...]).astype(BF16), v2_ref[...]))
    kkr = k * k_k
    ones = _pair_ones()
    kk = jnp.concatenate(
        [kkr[:, t:t + LANES] * lax.rsqrt(jnp.maximum(
            _head_sum(kkr[:, t:t + LANES] * kkr[:, t:t + LANES], ones, 2), 1e-12))
         for t in range(0, D_MODEL, LANES)], axis=1)
    r_ref[...] = r.astype(r_ref.dtype)
    k_ref[...] = (k * (1.0 + (a - 1.0) * k_a)).astype(k_ref.dtype)
    v_ref[...] = v.astype(v_ref.dtype)
    kk_ref[...] = kk.astype(kk_ref.dtype)
    ka_ref[...] = (kk * a).astype(ka_ref.dtype)


def _rwkv_proj_seq_kernel(*refs, tiles_per_seq, has_vres):
    x_ref, gn_ref = refs[0], refs[1]
    weights = refs[2:11]
    n_in = 11 + (3 if has_vres else 0)
    vres = refs[11:14] if has_vres else None
    outs = refs[n_in:n_in + 7]
    shift_ref, carry_ref = refs[n_in + 7], refs[n_in + 8]
    m = pl.program_id(0)

    @pl.when(m % tiles_per_seq == 0)
    def _():
        carry_ref[...] = jnp.zeros_like(carry_ref)

    h = _rms(x_ref[...], gn_ref[...])
    tm = h.shape[0]
    rows = lax.broadcasted_iota(jnp.int32, (tm, 1), 0)
    hprev = jnp.where(rows == 0, carry_ref[0:1, :], pltpu.roll(h, 1, axis=0))
    _rwkv_proj_body(h, hprev, *weights, vres, outs, BF16)
    carry_ref[0:1, :] = h[tm - 1:tm, :]
    shift_ref[0] = h[tm - 1:tm, :]


def _rwkv_proj_step_kernel(*refs, has_vres):
    x_ref, gn_ref, prev_ref = refs[0], refs[1], refs[2]
    weights = refs[3:12]
    n_in = 12 + (3 if has_vres else 0)
    vres = refs[12:15] if has_vres else None
    outs = refs[n_in:n_in + 7]
    shift_ref = refs[n_in + 7]
    h = _rms(x_ref[...], gn_ref[...])
    _rwkv_proj_body(h, prev_ref[...], *weights, vres, outs, F32)
    shift_ref[...] = h


def rwkv_proj(x, gn, prev, wts, vres, tm, n_seq, tiles_per_seq):
    m, d = x.shape
    full = lambda a: pl.BlockSpec(a.shape, lambda i, _n=a.ndim: (0,) * _n)
    rowblk = pl.BlockSpec((tm, d), lambda i: (i, 0))
    seq = prev is None
    ins = [x, gn] + ([] if seq else [prev]) + list(wts)
    specs = [rowblk, full(gn)] + ([] if seq else [rowblk]) + [full(a) for a in wts]
    if vres is not None:
        vf, v1, v2 = vres
        ins += [vf, v1, v2]
        specs += [rowblk, full(v1), full(v2)]
    out_shape = [jax.ShapeDtypeStruct((m, d), F32)] * 7
    out_specs = [rowblk] * 7
    if seq:
        out_shape.append(jax.ShapeDtypeStruct((n_seq, 1, d), F32))
        out_specs.append(pl.BlockSpec((1, 1, d), lambda i: (i // tiles_per_seq, 0, 0)))
        body = functools.partial(_rwkv_proj_seq_kernel, tiles_per_seq=tiles_per_seq, has_vres=vres is not None)
        scratch = [pltpu.VMEM((8, d), F32)]
    else:
        out_shape.append(jax.ShapeDtypeStruct((m, d), F32))
        out_specs.append(rowblk)
        body = functools.partial(_rwkv_proj_step_kernel, has_vres=vres is not None)
        scratch = []
    return pl.pallas_call(
        body,
        grid=(m // tm,),
        in_specs=specs,
        out_specs=out_specs,
        out_shape=out_shape,
        scratch_shapes=scratch,
        compiler_params=_params("arbitrary"),
        name="rwkv_proj_seq" if seq else "rwkv_proj_step",
    )(*ins)


def _wkv_epilogue(y, r, k, v, g, lnw, lnb, rk):
    inv_n = 1.0 / RWKV_HEAD
    first = lax.broadcasted_iota(jnp.int32, (1, LANES), 1) < RWKV_HEAD

    def head_sum(x):
        s0 = jnp.sum(jnp.where(first, x, 0.0), axis=-1, keepdims=True)
        s1 = jnp.sum(jnp.where(first, 0.0, x), axis=-1, keepdims=True)
        return jnp.where(first, s0, s1)

    yc = y - head_sum(y) * inv_n
    yn = yc * lax.rsqrt(head_sum(yc * yc) * inv_n + RWKV_GN_EPS)
    return (yn * lnw + lnb + head_sum(r * k * rk) * v) * g


def _stack_heads(x):
    first = lax.broadcasted_iota(jnp.int32, (1, LANES), 1) < RWKV_HEAD
    return jnp.concatenate([jnp.where(first, x, 0.0), jnp.where(first, 0.0, x)], axis=0)


def _wkv_seq_kernel(*refs, pairs, has_acc):
    r_ref, k_ref, v_ref, kk_ref, ka_ref, ld_ref, g_ref, lnw_ref, lnb_ref, rk_ref = refs[:10]
    z_ref, s_out_ref, s_ref = refs[10 + has_acc:]
    c = pl.program_id(1)
    L = WKV_CHUNK
    R = 2 * L

    @pl.when(c == 0)
    def _():
        s_ref[...] = jnp.zeros_like(s_ref)

    dot = lambda a, b: jnp.dot(a.astype(BF16), b.astype(BF16), preferred_element_type=F32)
    dot_nt = lambda a, b: lax.dot_general(a.astype(BF16), b.astype(BF16), NT_DIMS, preferred_element_type=F32)
    dot_tn = lambda a, b: lax.dot_general(a.astype(BF16), b.astype(BF16), TN_DIMS, preferred_element_type=F32)
    ti = lax.broadcasted_iota(jnp.int32, (L, L), 0)
    tj = lax.broadcasted_iota(jnp.int32, (L, L), 1)
    tri = jnp.where(ti >= tj, 1.0, 0.0).astype(BF16)
    ri = lax.broadcasted_iota(jnp.int32, (R, R), 0)
    rj = lax.broadcasted_iota(jnp.int32, (R, R), 1)
    same = (ri // L) == (rj // L)
    lower = same & ((ri % L) > (rj % L))
    lower_eq = same & ((ri % L) >= (rj % L))
    eye = jnp.where(ri == rj, 1.0, 0.0)

    for first in range(0, pairs, WKV_GROUP):
        _wkv_group(range(first, first + WKV_GROUP), refs[:10], z_ref, s_out_ref, s_ref, c,
                   (dot, dot_nt, dot_tn, tri, lower, lower_eq, eye))

    @pl.when(c == pl.num_programs(1) - 1)
    def _():
        for p in range(pairs):
            s_out_ref[0, 0, 2 * p] = s_ref[p, 0:RWKV_HEAD, 0:RWKV_HEAD]
            s_out_ref[0, 0, 2 * p + 1] = s_ref[p, RWKV_HEAD:LANES, RWKV_HEAD:LANES]


def _wkv_group(P, ins, z_ref, s_out_ref, s_ref, c, consts):
    r_ref, k_ref, v_ref, kk_ref, ka_ref, ld_ref, g_ref, lnw_ref, lnb_ref, rk_ref = ins
    dot, dot_nt, dot_tn, tri, lower, lower_eq, eye = consts
    L = WKV_CHUNK
    R = 2 * L
    P = list(P)
    sls = {p: slice(p * LANES, (p + 1) * LANES) for p in P}
    ld = {p: ld_ref[:, sls[p]] for p in P}
    ld_hi = {p: ld[p].astype(BF16) for p in P}
    ld_mid = {p: (ld[p] - ld_hi[p].astype(F32)).astype(BF16) for p in P}
    ld_lo = {p: (ld[p] - ld_hi[p].astype(F32) - ld_mid[p].astype(F32)).astype(BF16) for p in P}
    cum = {p: dot(tri, ld_hi[p]) + dot(tri, ld_mid[p]) + dot(tri, ld_lo[p]) for p in P}
    tot = {p: cum[p][L - 1:L, :] for p in P}
    ar, bk, bk_end, v_s = {}, {}, {}, {}
    for p in P:
        r, k, v, kk, ka = (ref[:, sls[p]] for ref in (r_ref, k_ref, v_ref, kk_ref, ka_ref))
        dec_out = jnp.exp(-cum[p])
        dec_end = jnp.exp(tot[p] - cum[p])
        ar[p] = jnp.concatenate([_stack_heads(jnp.exp(cum[p] - ld[p]) * (-kk)),
                                 _stack_heads(jnp.exp(cum[p]) * r)], axis=0).astype(BF16)
        bk[p] = jnp.concatenate([_stack_heads(dec_out * ka), _stack_heads(dec_out * k)], axis=0).astype(BF16)
        bk_end[p] = jnp.concatenate([_stack_heads(dec_end * ka), _stack_heads(dec_end * k)], axis=0).astype(BF16)
        v_s[p] = _stack_heads(v)
    big = {p: dot_nt(ar[p], bk[p]) for p in P}
    n_ab = {p: jnp.where(lower, big[p][0:R, 0:R], 0.0) for p in P}
    a_ak = {p: jnp.where(lower, big[p][0:R, R:2 * R], 0.0).astype(BF16) for p in P}
    a_r = {p: jnp.concatenate([jnp.where(lower_eq, big[p][R:2 * R, 0:R], 0.0),
                               jnp.where(lower_eq, big[p][R:2 * R, R:2 * R], 0.0)], axis=1).astype(BF16)
           for p in P}
    inv = {p: eye + n_ab[p] for p in P}
    pw = {p: dot(n_ab[p], n_ab[p]) for p in P}
    for _ in range(int(math.log2(L)) - 2):
        both = {p: dot(jnp.concatenate([pw[p], inv[p]], axis=0), pw[p]) for p in P}
        inv = {p: inv[p] + both[p][R:2 * R] for p in P}
        pw = {p: both[p][0:R] for p in P}
    inv = {p: inv[p] + dot(inv[p], pw[p]) for p in P}
    s_old = {p: s_ref[p] for p in P}
    ars = {p: dot_nt(ar[p], s_old[p]) for p in P}
    akv = {p: dot(a_ak[p], v_s[p]) for p in P}
    u_s = {p: dot(inv[p], ars[p][0:R] + akv[p]) for p in P}
    uv = {p: jnp.concatenate([u_s[p], v_s[p]], axis=0).astype(BF16) for p in P}
    y_s = {p: ars[p][R:2 * R] + dot(a_r[p], uv[p]) for p in P}
    s_new = {p: s_old[p] * jnp.exp(tot[p]) + dot_tn(uv[p], bk_end[p]) for p in P}
    for p in P:
        s_ref[p] = s_new[p]
    for p in P:
        sl = sls[p]
        y = y_s[p][0:L, :] + y_s[p][L:R, :]
        z_ref[:, sl] = _wkv_epilogue(y, r_ref[:, sl], k_ref[:, sl], v_ref[:, sl], g_ref[:, sl], lnw_ref[:, sl],
                                     lnb_ref[:, sl], rk_ref[:, sl]).astype(z_ref.dtype)


def wkv_seq(r, k, v, kk, ka, ld, g, lnw, lnb, rk, n_seq, chunks, layer, n_layers, acc):
    m, d = r.shape
    L = WKV_CHUNK
    blk = pl.BlockSpec((L, d), lambda b, c: (b * chunks + c, 0))
    vec = pl.BlockSpec((1, d), lambda b, c: (0, 0))
    ins = [r, k, v, kk, ka, ld, g, lnw, lnb, rk]
    specs = [blk] * 7 + [vec] * 3
    aliases = _layer_slab(acc, ins, specs, 1)
    return pl.pallas_call(
        functools.partial(_wkv_seq_kernel, pairs=d // LANES, has_acc=acc is not None),
        grid=(n_seq, chunks),
        in_specs=specs,
        out_specs=[blk, pl.BlockSpec((1, 1, RWKV_HEADS, RWKV_HEAD, RWKV_HEAD), lambda b, c: (layer, b, 0, 0, 0))],
        out_shape=[jax.ShapeDtypeStruct((m, d), BF16),
                   jax.ShapeDtypeStruct((n_layers, n_seq, RWKV_HEADS, RWKV_HEAD, RWKV_HEAD), F32)],
        scratch_shapes=[pltpu.VMEM((d // LANES, LANES, LANES), F32)],
        input_output_aliases=aliases,
        compiler_params=_params("parallel", "arbitrary"),
        name="wkv_seq",
    )(*ins)


def _wkv_step_kernel(*refs, has_acc):
    r_ref, k_ref, v_ref, kk_ref, ka_ref, ld_ref, g_ref, lnw_ref, lnb_ref, rk_ref, s_ref = refs[:11]
    z_ref, s_out_ref, vt_ref, y_ref = refs[11 + has_acc:]
    N = RWKV_HEAD
    SUB = 8
    r, k, v, nkk, ka, g = (ref[...].T for ref in (r_ref, k_ref, v_ref, kk_ref, ka_ref, g_ref))
    nkk = -nkk
    w = jnp.exp(ld_ref[...].T)
    vt_ref[...] = v
    sub = lax.broadcasted_iota(jnp.int32, (SUB, 1), 0)
    for hh in range(2):
        rows = slice(hh * N, (hh + 1) * N)
        r_h, k_h, nkk_h, ka_h, w_h = r[rows], k[rows], nkk[rows], ka[rows], w[rows]

        def value_rows(blk, carry, hh=hh, r_h=r_h, k_h=k_h, nkk_h=nkk_h, ka_h=ka_h, w_h=w_h):
            base = pl.multiple_of(hh * N + blk * SUB, SUB)
            v_blk = vt_ref[pl.ds(base, SUB), :]
            y_blk = jnp.zeros((SUB, v_blk.shape[1]), F32)
            for ii in range(SUB):
                s_old = s_ref[0, hh, blk * SUB + ii]
                sa = jnp.sum(s_old * nkk_h, axis=0, keepdims=True)
                s_new = s_old * w_h + sa * ka_h + v_blk[ii:ii + 1, :] * k_h
                s_out_ref[0, hh, blk * SUB + ii] = s_new
                y_blk = jnp.where(sub == ii, jnp.sum(s_new * r_h, axis=0, keepdims=True), y_blk)
            y_ref[pl.ds(base, SUB), :] = y_blk
            return carry

        lax.fori_loop(0, N // SUB, value_rows, 0)

    def per_head(x, op):
        return jnp.concatenate([jnp.broadcast_to(op(x[hh * N:(hh + 1) * N], axis=0, keepdims=True), (N, x.shape[1]))
                                for hh in range(2)], axis=0)

    y = y_ref[...]
    yc = y - per_head(y, jnp.mean)
    yn = yc * lax.rsqrt(per_head(yc * yc, jnp.mean) + RWKV_GN_EPS)
    bonus = per_head(r * k * rk_ref[...], jnp.sum) * v
    z_ref[...] = ((yn * lnw_ref[...] + lnb_ref[...] + bonus) * g).T


def wkv_step(r, k, v, kk, ka, ld, g, lnw, lnb, rk, states_t, layer, acc):
    n, d = r.shape
    tile = pl.BlockSpec((n, LANES), lambda p: (0, p))
    col = pl.BlockSpec((LANES, 1), lambda p: (p, 0))
    st = pl.BlockSpec((1, 2) + states_t.shape[2:], lambda p: (layer, p, 0, 0, 0))
    ins = [r, k, v, kk, ka, ld, g, lnw, lnb, rk, states_t]
    specs = [tile] * 7 + [col] * 3 + [st]
    aliases = _layer_slab(acc, ins, specs, 1)
    return pl.pallas_call(
        functools.partial(_wkv_step_kernel, has_acc=acc is not None),
        grid=(d // LANES,),
        in_specs=specs,
        out_specs=[tile, st],
        out_shape=[jax.ShapeDtypeStruct((n, d), F32), jax.ShapeDtypeStruct(states_t.shape, F32)],
        scratch_shapes=[pltpu.VMEM((LANES, n), F32), pltpu.VMEM((LANES, n), F32)],
        input_output_aliases=aliases,
        compiler_params=_params("parallel"),
        name="wkv_step",
    )(*ins)


def _trunk(x, seq, states, p):
    depth = p['norm_mix'].shape[0]
    rows = x.shape[0]
    if seq is not None:
        n_seq, tp, lead = seq
        tm = _row_tile(tp, 1088)
        tm_small = _row_tile(tp, 544)
        tm_rwkv = _row_tile(tp, 272)
        tiles_small, tiles_rwkv = tp // tm_small, tp // tm_rwkv
        cos, sin = rot_table(tp, -lead, 1)
    else:
        ret_s, wkv_s, shift_s, conv_s = states
        wkv_s = jnp.transpose(wkv_s, (0, 2, 3, 4, 1))
        tm = tm_small = rows
        tiles_small = 1
        lead = 0
        cos, sin = rot_table(8, PAST_LEN, 0)
    new_ret = new_wkv = None
    new_shift, new_conv = [], []
    n_ret, n_rwkv = (depth + 1) // 2, depth // 2
    v_first = None
    for i in range(depth):
        j = i // 2
        gn = p['norm_mix'][i][None]
        if i % 2 == 0:
            gnw = p['ret_gn_w'][j][None]
            if seq is not None:
                proj = norm_proj(x, gn, p['ret_w_in'][j], tm, 1024, BF16)
                o, new_ret = retention_prompt(proj, cos, sin, gnw, n_seq, tp // RET_CHUNK, j, n_ret, new_ret)
            else:
                proj = norm_proj(x, gn, p['ret_w_in'][j], tm, 1024, F32)
                o, new_ret = retention_step(proj[:, None, :], cos, sin, gnw, ret_s, j, new_ret)
                o = o[:, 0, :]
            x = matmul_res(o, p['ret_w_out'][j], x, tm_small, tiles_small, lead)
        else:
            vecs = [p['rwkv_w0'][j], p['rwkv_a0'][j], p['rwkv_k_k'][j], p['rwkv_k_a'][j]]
            vecs.append(p['rwkv_v0'][j - 1] if j else jnp.zeros_like(vecs[0]))
            vecs = jnp.stack(vecs + [jnp.zeros_like(vecs[0])] * 3)
            wts = [p['rwkv_mu'][j], p['rwkv_w_rkv'][j], p['rwkv_w1'][j], p['rwkv_w2'][j], p['rwkv_a1'][j],
                   p['rwkv_a2'][j], p['rwkv_g1'][j], p['rwkv_g2'][j], vecs]
            vres = (v_first, p['rwkv_v1'][j - 1], p['rwkv_v2'][j - 1]) if j else None
            lnw, lnb = p['rwkv_ln_w'][j][None], p['rwkv_ln_b'][j][None]
            rk = p['rwkv_r_k'][j].reshape(1, D_MODEL)
            if seq is not None:
                r, k, v, kk, ka, ld, g, sh = rwkv_proj(x, gn, None, wts, vres, tm_rwkv, n_seq, tiles_rwkv)
                z, new_wkv = wkv_seq(r, k, v, kk, ka, ld, g, lnw, lnb, rk, n_seq, tp // WKV_CHUNK,
                                     j, n_rwkv, new_wkv)
                sh = sh[:, 0, :]
            else:
                r, k, v, kk, ka, ld, g, sh = rwkv_proj(x, gn, shift_s[j], wts, vres, tm, 1, 1)
                z, new_wkv = wkv_step(r, k, v, kk, ka, ld, g, *(t.reshape(D_MODEL, 1) for t in (lnw, lnb, rk)),
                                      wkv_s, j, new_wkv)
            if v_first is None:
                v_first = v
            new_shift.append(sh)
            x = matmul_res(z, p['rwkv_w_o'][j], x, tm_small, tiles_small, lead)
        gf = p['norm_ffn'][i][None]
        cw, cb = p['ffn_conv_w'][i], p['ffn_conv_b'][i][None]
        if seq is not None:
            a, cbuf = ffn_up_seq(x, gf, p['ffn_w_ug'][i], cw, cb, tm_small, n_seq, tiles_small)
        else:
            a, n0, n1 = ffn_up_step(x, gf, p['ffn_w_ug'][i], cw, cb, conv_s[i].reshape(rows, 2 * D_FF), D_FF // 2)
            cbuf = jnp.stack([n0, n1], axis=1)
        new_conv.append(cbuf)
        x = matmul_res(a, p['ffn_w_d'][i], x, tm_small, tiles_small, lead)
    if seq is None:
        new_wkv = jnp.transpose(new_wkv, (0, 4, 1, 2, 3))
    return x, new_ret, new_wkv, jnp.stack(new_shift), jnp.stack(new_conv)


def kernel(x_prompt, x_sample, state_ret, state_wkv, state_shift, state_conv, meta_tokens, norm_mix, norm_ffn, norm_final, ret_w_in, ret_gn_w, ret_w_out, rwkv_mu, rwkv_w_rkv, rwkv_w0, rwkv_w1, rwkv_w2, rwkv_a0, rwkv_a1, rwkv_a2, rwkv_v0, rwkv_v1, rwkv_v2, rwkv_g1, rwkv_g2, rwkv_k_k, rwkv_k_a, rwkv_r_k, rwkv_ln_w, rwkv_ln_b, rwkv_w_o, ffn_w_ug, ffn_conv_w, ffn_conv_b, ffn_w_d):
    bf = lambda w: w.astype(BF16)
    p = dict(norm_mix=norm_mix, norm_ffn=norm_ffn, ret_w_in=bf(ret_w_in), ret_gn_w=ret_gn_w,
             ret_w_out=bf(ret_w_out), rwkv_mu=rwkv_mu, rwkv_w_rkv=bf(rwkv_w_rkv), rwkv_w0=rwkv_w0,
             rwkv_w1=bf(rwkv_w1), rwkv_w2=bf(rwkv_w2), rwkv_a0=rwkv_a0, rwkv_a1=bf(rwkv_a1),
             rwkv_a2=bf(rwkv_a2), rwkv_v0=rwkv_v0, rwkv_v1=bf(rwkv_v1), rwkv_v2=bf(rwkv_v2),
             rwkv_g1=bf(rwkv_g1), rwkv_g2=bf(rwkv_g2), rwkv_k_k=rwkv_k_k, rwkv_k_a=rwkv_k_a,
             rwkv_r_k=rwkv_r_k, rwkv_ln_w=rwkv_ln_w, rwkv_ln_b=rwkv_ln_b, rwkv_w_o=bf(rwkv_w_o),
             ffn_w_ug=bf(ffn_w_ug), ffn_conv_w=ffn_conv_w, ffn_conv_b=ffn_conv_b, ffn_w_d=bf(ffn_w_d))
    B, S, D = x_prompt.shape
    lead = (-N_META) % RET_CHUNK
    tp = lead + N_META + S
    assert tp % RET_CHUNK == 0 and D == D_MODEL
    meta = jnp.broadcast_to(meta_tokens[None].astype(F32), (B, N_META, D))
    xp = jnp.concatenate([jnp.zeros((B, lead, D), F32), meta, x_prompt.astype(F32)], axis=1).reshape(B * tp, D)
    xp, p_ret, p_wkv, p_shift, p_conv = _trunk(xp, (B, tp, lead), None, p)
    skip = (lead + N_META) // RET_CHUNK
    y_prompt = final_norm(xp, norm_final[None], B, tp // RET_CHUNK, skip, RET_CHUNK).reshape(B, S, D)

    n_req = x_sample.shape[0]
    xs, s_ret, s_wkv, s_shift, s_conv = _trunk(
        x_sample.reshape(n_req, D).astype(F32), None, (state_ret, state_wkv, state_shift, state_conv), p)
    y_sample = final_norm(xs, norm_final[None], 1, 1, 0, n_req).reshape(n_req, 1, D)
    return (y_prompt.astype(x_prompt.dtype), y_sample.astype(x_sample.dtype),
            p_ret, p_wkv, p_shift, p_conv, s_ret, s_wkv, s_shift, s_conv)
```

```python
import functools
import math

import jax
import jax.numpy as jnp
from jax import lax
from jax.experimental import pallas as pl
from jax.experimental.pallas import tpu as pltpu

F32 = jnp.float32
BF16 = jnp.bfloat16

D_MODEL = 1024
N_META = 16
PAST_LEN = 16384
RET_HEADS = 4
RET_DK = D_MODEL // RET_HEADS
RET_DV = 2 * RET_DK
RET_VDIM = RET_HEADS * RET_DV
RET_CHUNK = 128
RWKV_HEAD = 64
RWKV_HEADS = D_MODEL // RWKV_HEAD
D_FF = 2816
RMS_EPS = 1e-6
RET_GN_EPS = 1e-5
RWKV_GN_EPS = 64e-5

LANES = 128
WKV_CHUNK = 64
WKV_GROUP = 8
VMEM_LIMIT = 56 * 1024 * 1024

NT_DIMS = (((1,), (1,)), ((), ()))
TN_DIMS = (((0,), (0,)), ((), ()))


def _params(*sem):
    return pltpu.CompilerParams(dimension_semantics=sem, vmem_limit_bytes=VMEM_LIMIT)


def _rms(x, g):
    return x * lax.rsqrt(jnp.mean(x * x, axis=-1, keepdims=True) + RMS_EPS) * g


def _sigmoid(x):
    return 1.0 / (1.0 + jnp.exp(-x))


def _row_tile(rows_per_seq, cap):
    best = None
    for t in range(16, min(rows_per_seq, cap) + 1, 16):
        if rows_per_seq % t == 0:
            best = t
    assert best is not None, rows_per_seq
    return best


def _rot_table_kernel(inv_ref, cos_ref, sin_ref, *, pos0, step):
    rows = lax.broadcasted_iota(jnp.int32, cos_ref.shape, 0)
    pos = (rows * step + pos0).astype(F32)
    ang = pos * inv_ref[...]
    cos_ref[...] = jnp.cos(ang)
    sin_ref[...] = jnp.sin(ang)


def rot_table(n_rows, pos0, step):
    half = RET_DK // 2
    inv = (1.0 / (10000.0 ** jnp.linspace(0.0, 1.0, half, dtype=F32))).reshape(1, half)
    return pl.pallas_call(
        functools.partial(_rot_table_kernel, pos0=pos0, step=step),
        out_shape=(jax.ShapeDtypeStruct((n_rows, half), F32),) * 2,
        name="rot_table",
    )(inv)


def _norm_proj_kernel(x_ref, g_ref, w_ref, o_ref, h_ref):
    @pl.when(pl.program_id(1) == 0)
    def _():
        h_ref[...] = _rms(x_ref[...], g_ref[...]).astype(BF16)

    o_ref[...] = jnp.dot(h_ref[...], w_ref[...], preferred_element_type=F32).astype(o_ref.dtype)


def norm_proj(x, g, w, tm, tn, out_dtype):
    m, d = x.shape
    n = w.shape[1]
    return pl.pallas_call(
        _norm_proj_kernel,
        grid=(m // tm, n // tn),
        in_specs=[pl.BlockSpec((tm, d), lambda i, j: (i, 0)),
                  pl.BlockSpec((1, d), lambda i, j: (0, 0)),
                  pl.BlockSpec((d, tn), lambda i, j: (0, j))],
        out_specs=pl.BlockSpec((tm, tn), lambda i, j: (i, j)),
        out_shape=jax.ShapeDtypeStruct((m, n), out_dtype),
        scratch_shapes=[pltpu.VMEM((tm, d), BF16)],
        compiler_params=_params("parallel", "arbitrary"),
        name="norm_proj",
    )(x, g, w)


def _ret_proj_kernel(x_ref, g_ref, w_ref, cos_ref, sin_ref, o_ref, h_ref):
    j = pl.program_id(1)
    half = RET_DK // 2

    @pl.when(j == 0)
    def _():
        h_ref[...] = _rms(x_ref[...], g_ref[...]).astype(BF16)

    def head_cols(hd):
        return jnp.dot(h_ref[...], w_ref[:, hd * RET_DK:(hd + 1) * RET_DK], preferred_element_type=F32)

    @pl.when(j < 2)
    def _():
        scale = jnp.where(j == 1, RET_DK ** -0.5, 1.0)
        cos, sin = cos_ref[...] * scale, sin_ref[...] * scale
        for hd in range(RET_HEADS):
            x = head_cols(hd)
            x1, x2 = x[:, :half], x[:, half:]
            o_ref[:, hd * RET_DK:hd * RET_DK + half] = (x1 * cos - x2 * sin).astype(o_ref.dtype)
            o_ref[:, hd * RET_DK + half:(hd + 1) * RET_DK] = (x1 * sin + x2 * cos).astype(o_ref.dtype)

    @pl.when((j >= 2) & (j < 4))
    def _():
        for hd in range(RET_HEADS):
            o_ref[:, hd * RET_DK:(hd + 1) * RET_DK] = head_cols(hd).astype(o_ref.dtype)

    @pl.when(j >= 4)
    def _():
        for hd in range(RET_HEADS):
            x = head_cols(hd)
            o_ref[:, hd * RET_DK:(hd + 1) * RET_DK] = (x * _sigmoid(x)).astype(o_ref.dtype)


def ret_proj(x, g, w, cos, sin, tm, tiles_per_seq):
    m, d = x.shape
    n = w.shape[1]
    tn = RET_HEADS * RET_DK
    assert n == 2 * tn + 2 * RET_VDIM and RET_VDIM == 2 * tn
    return pl.pallas_call(
        _ret_proj_kernel,
        grid=(m // tm, n // tn),
        in_specs=[pl.BlockSpec((tm, d), lambda i, j: (i, 0)),
                  pl.BlockSpec((1, d), lambda i, j: (0, 0)),
                  pl.BlockSpec((d, tn), lambda i, j: (0, j)),
                  pl.BlockSpec((tm, RET_DK // 2), lambda i, j: (i % tiles_per_seq, 0)),
                  pl.BlockSpec((tm, RET_DK // 2), lambda i, j: (i % tiles_per_seq, 0))],
        out_specs=pl.BlockSpec((tm, tn), lambda i, j: (i, j)),
        out_shape=jax.ShapeDtypeStruct((m, n), BF16),
        scratch_shapes=[pltpu.VMEM((tm, d), BF16)],
        compiler_params=_params("parallel", "arbitrary"),
        name="ret_proj",
    )(x, g, w, cos, sin)


def _matmul_res_kernel(a_ref, w_ref, x_ref, o_ref, *, tiles_per_seq, lead):
    y = x_ref[...] + jnp.dot(a_ref[...].astype(BF16), w_ref[...], preferred_element_type=F32)
    if lead:
        tm = y.shape[0]
        row = (pl.program_id(0) % tiles_per_seq) * tm + lax.broadcasted_iota(jnp.int32, (tm, 1), 0)
        y = jnp.where(row >= lead, y, 0.0)
    o_ref[...] = y


def matmul_res(a, w, x, tm, tiles_per_seq, lead):
    m, k = a.shape
    n = w.shape[1]
    return pl.pallas_call(
        functools.partial(_matmul_res_kernel, tiles_per_seq=tiles_per_seq, lead=lead),
        grid=(m // tm,),
        in_specs=[pl.BlockSpec((tm, k), lambda i: (i, 0)),
                  pl.BlockSpec((k, n), lambda i: (0, 0)),
                  pl.BlockSpec((tm, n), lambda i: (i, 0))],
        out_specs=pl.BlockSpec((tm, n), lambda i: (i, 0)),
        out_shape=jax.ShapeDtypeStruct((m, n), F32),
        compiler_params=_params("parallel"),
        name="matmul_res",
    )(a, w, x)


def _final_norm_kernel(x_ref, g_ref, o_ref, *, skip):
    o_ref[0] = _rms(x_ref[0, skip:, :], g_ref[...])


def final_norm(x, g, skip):
    n_seq, rows, d = x.shape
    return pl.pallas_call(
        functools.partial(_final_norm_kernel, skip=skip),
        grid=(n_seq,),
        in_specs=[pl.BlockSpec((1, rows, d), lambda b: (b, 0, 0)),
                  pl.BlockSpec((1, d), lambda b: (0, 0))],
        out_specs=pl.BlockSpec((1, rows - skip, d), lambda b: (b, 0, 0)),
        out_shape=jax.ShapeDtypeStruct((n_seq, rows - skip, d), F32),
        compiler_params=_params("parallel"),
        name="final_norm",
    )(x, g)


def _rotate(x, cos, sin):
    half = RET_DK // 2
    x1, x2 = x[:, :half], x[:, half:]
    return jnp.concatenate([x1 * cos - x2 * sin, x1 * sin + x2 * cos], axis=1)


def _gated_head_norm(o, silu_g, gnw):
    oc = o - jnp.mean(o, axis=-1, keepdims=True)
    on = oc * lax.rsqrt(jnp.mean(oc * oc, axis=-1, keepdims=True) + RET_GN_EPS)
    return silu_g * (on * gnw)


def _log_g(h):
    return math.log(1.0 - 2.0 ** (-5.0 - h))


def _layer_slab(acc, ins, specs, out_index):
    if acc is None:
        return {}
    ins.append(acc)
    specs.append(pl.BlockSpec(memory_space=pl.ANY))
    return {len(ins) - 1: out_index}


def _slab_spec(acc, layer, n_layers, rest_block, rest_index):
    if acc is None:
        return pl.BlockSpec((n_layers,) + rest_block, lambda *g: (0,) + rest_index(*g)), layer, n_layers
    return pl.BlockSpec((1,) + rest_block, lambda *g: (layer,) + rest_index(*g)), 0, 1


def _zero_other_slabs(s_out_ref, slab, n_slabs):
    for other in range(n_slabs):
        if other != slab:
            s_out_ref[other] = jnp.zeros(s_out_ref.shape[1:], s_out_ref.dtype)


def _retention_kernel(*refs, has_acc, slab, n_slabs):
    p_ref, gnw_ref = refs[:2]
    o_ref, s_out_ref, s_ref = refs[2 + has_acc:]
    c = pl.program_id(1)
    L = RET_CHUNK
    heads = range(RET_HEADS)
    qk = RET_HEADS * RET_DK

    @pl.when(c == 0)
    def _():
        s_ref[...] = jnp.zeros_like(s_ref)

    dot = lambda a, b: jnp.dot(a, b, preferred_element_type=F32)
    ri = lax.broadcasted_iota(jnp.int32, (L, 1), 0).astype(F32)
    ci = lax.broadcasted_iota(jnp.int32, (1, L), 1).astype(F32)
    rel = ri - ci
    lg = [_log_g(h) for h in heads]
    q = [p_ref[:, h * RET_DK:(h + 1) * RET_DK] for h in heads]
    k = [p_ref[:, qk + h * RET_DK:qk + (h + 1) * RET_DK] for h in heads]
    v = [p_ref[:, 2 * qk + h * RET_DV:2 * qk + (h + 1) * RET_DV] for h in heads]
    scores = [lax.dot_general(q[h], k[h], NT_DIMS, preferred_element_type=F32)
              * jnp.where(rel >= 0, jnp.exp(lg[h] * jnp.maximum(rel, 0.0)), 0.0) for h in heads]
    qs = [dot(q[h], s_ref[h].astype(BF16)) for h in heads]
    o = [dot(scores[h].astype(BF16), v[h]) + jnp.exp(lg[h] * (ri + 1.0)) * qs[h] for h in heads]
    kd = [(k[h].astype(F32) * jnp.exp(lg[h] * (L - 1.0 - ri))).astype(BF16) for h in heads]
    for h in heads:
        s_ref[h] = (math.exp(lg[h] * L) * s_ref[h]
                    + lax.dot_general(kd[h], v[h], TN_DIMS, preferred_element_type=F32))
    g0 = 2 * qk + RET_VDIM
    for h in heads:
        sl = slice(h * RET_DV, (h + 1) * RET_DV)
        silu_g = p_ref[:, g0 + h * RET_DV:g0 + (h + 1) * RET_DV].astype(F32)
        o_ref[:, sl] = _gated_head_norm(o[h], silu_g, gnw_ref[:, sl]).astype(o_ref.dtype)

    @pl.when(c == pl.num_programs(1) - 1)
    def _():
        s_out_ref[slab, 0] = s_ref[...]
        _zero_other_slabs(s_out_ref, slab, n_slabs)


def retention_prompt(proj, gnw, n_seq, chunks, layer, n_layers, acc):
    L = RET_CHUNK
    m, width = proj.shape
    ins = [proj, gnw]
    specs = [pl.BlockSpec((L, width), lambda b, c: (b * chunks + c, 0)),
             pl.BlockSpec((1, RET_VDIM), lambda b, c: (0, 0))]
    aliases = _layer_slab(acc, ins, specs, 1)
    st, slab, n_slabs = _slab_spec(acc, layer, n_layers, (1, RET_HEADS, RET_DK, RET_DV), lambda b, c: (b, 0, 0, 0))
    return pl.pallas_call(
        functools.partial(_retention_kernel, has_acc=acc is not None, slab=slab, n_slabs=n_slabs),
        grid=(n_seq, chunks),
        in_specs=specs,
        out_specs=[pl.BlockSpec((L, RET_VDIM), lambda b, c: (b * chunks + c, 0)), st],
        out_shape=[jax.ShapeDtypeStruct((m, RET_VDIM), BF16),
                   jax.ShapeDtypeStruct((n_layers, n_seq, RET_HEADS, RET_DK, RET_DV), F32)],
        scratch_shapes=[pltpu.VMEM((RET_HEADS, RET_DK, RET_DV), F32)],
        input_output_aliases=aliases,
        compiler_params=_params("parallel", "arbitrary"),
        name="retention_prompt",
    )(*ins)


def _to_column(row, n):
    eye = lax.broadcasted_iota(jnp.int32, (n, n), 0) == lax.broadcasted_iota(jnp.int32, (n, n), 1)
    return jnp.sum(jnp.where(eye, jnp.broadcast_to(row, (n, n)), 0.0), axis=-1, keepdims=True)


def _retention_step_kernel(*refs, has_acc, slab, n_slabs):
    p_ref, cos_ref, sin_ref, gnw_ref, s_ref = refs[:5]
    o_ref, s_out_ref = refs[5 + has_acc:]
    _zero_other_slabs(s_out_ref, slab, n_slabs)
    cos, sin = cos_ref[0:1, :], sin_ref[0:1, :]
    qk = RET_HEADS * RET_DK
    for h in range(RET_HEADS):
        decay = math.exp(_log_g(h))
        q = _rotate(p_ref[0, :, h * RET_DK:(h + 1) * RET_DK], cos, sin)
        k = _rotate(p_ref[0, :, qk + h * RET_DK:qk + (h + 1) * RET_DK], cos, sin) * (RET_DK ** -0.5)
        v = p_ref[0, :, 2 * qk + h * RET_DV:2 * qk + (h + 1) * RET_DV]
        g = p_ref[0, :, 2 * qk + RET_VDIM + h * RET_DV:2 * qk + RET_VDIM + (h + 1) * RET_DV]
        s_old = s_ref[0, 0, h]
        qs = jnp.sum(s_old * _to_column(q, RET_DK), axis=0, keepdims=True)
        o = jnp.sum(q * k, axis=-1, keepdims=True) * v + decay * qs
        s_out_ref[slab, 0, h] = decay * s_old + _to_column(k, RET_DK) * v
        o_ref[0, :, h * RET_DV:(h + 1) * RET_DV] = _gated_head_norm(
            o, g * _sigmoid(g), gnw_ref[:, h * RET_DV:(h + 1) * RET_DV]).astype(o_ref.dtype)


def retention_step(proj, cos, sin, gnw, states, layer, acc):
    n = proj.shape[0]
    st = pl.BlockSpec((1, 1, RET_HEADS, RET_DK, RET_DV), lambda b: (layer, b, 0, 0, 0))
    ins = [proj, cos, sin, gnw, states]
    specs = [pl.BlockSpec((1, 1, proj.shape[2]), lambda b: (b, 0, 0)),
             pl.BlockSpec(cos.shape, lambda b: (0, 0)),
             pl.BlockSpec(sin.shape, lambda b: (0, 0)),
             pl.BlockSpec((1, RET_VDIM), lambda b: (0, 0)),
             st]
    aliases = _layer_slab(acc, ins, specs, 1)
    st_out, slab, n_slabs = _slab_spec(acc, layer, states.shape[0], (1, RET_HEADS, RET_DK, RET_DV),
                                       lambda b: (b, 0, 0, 0))
    return pl.pallas_call(
        functools.partial(_retention_step_kernel, has_acc=acc is not None, slab=slab, n_slabs=n_slabs),
        grid=(n,),
        in_specs=specs,
        out_specs=[pl.BlockSpec((1, 1, RET_VDIM), lambda b: (b, 0, 0)), st_out],
        out_shape=[jax.ShapeDtypeStruct((n, 1, RET_VDIM), BF16),
                   jax.ShapeDtypeStruct(states.shape, F32)],
        input_output_aliases=aliases,
        compiler_params=_params("parallel"),
        name="retention_step",
    )(*ins)


FFN_COLS = 256


def _conv_gate(gate, p1, p2, cw, cb):
    rows = lax.broadcasted_iota(jnp.int32, (gate.shape[0], 1), 0)
    g1 = jnp.where(rows == 0, p1, pltpu.roll(gate, 1, axis=0))
    g2 = jnp.where(rows == 0, p2, jnp.where(rows == 1, p1, pltpu.roll(gate, 2, axis=0)))
    return cb + g2 * cw[0:1, :] + g1 * cw[1:2, :] + gate * cw[2:3, :]


def _ffn_up_seq_kernel(x_ref, g_ref, w_ref, cw_ref, cb_ref, a_ref, nc_ref, carry_ref, *, tiles_per_seq):
    @pl.when(pl.program_id(0) % tiles_per_seq == 0)
    def _():
        carry_ref[...] = jnp.zeros_like(carry_ref)

    h = _rms(x_ref[...], g_ref[...]).astype(BF16)
    tm = h.shape[0]
    for j in range(D_FF // FFN_COLS):
        sl = slice(j * FFN_COLS, (j + 1) * FFN_COLS)
        u = jnp.dot(h, w_ref[:, sl], preferred_element_type=F32)
        gate = jnp.dot(h, w_ref[:, D_FF + j * FFN_COLS:D_FF + (j + 1) * FFN_COLS], preferred_element_type=F32)
        conv = _conv_gate(gate, carry_ref[1:2, sl], carry_ref[0:1, sl], cw_ref[:, sl], cb_ref[:, sl])
        a_ref[:, sl] = (conv * _sigmoid(conv) * u).astype(a_ref.dtype)
        last = gate[tm - 2:tm, :]
        carry_ref[0:2, sl] = last
        nc_ref[0, :, sl] = last


def ffn_up_seq(x, g, w_ug, cw, cb, tm, n_seq, tiles_per_seq):
    m, d = x.shape
    return pl.pallas_call(
        functools.partial(_ffn_up_seq_kernel, tiles_per_seq=tiles_per_seq),
        grid=(m // tm,),
        in_specs=[pl.BlockSpec((tm, d), lambda i: (i, 0)),
                  pl.BlockSpec((1, d), lambda i: (0, 0)),
                  pl.BlockSpec(w_ug.shape, lambda i: (0, 0)),
                  pl.BlockSpec(cw.shape, lambda i: (0, 0)),
                  pl.BlockSpec(cb.shape, lambda i: (0, 0))],
        out_specs=[pl.BlockSpec((tm, D_FF), lambda i: (i, 0)),
                   pl.BlockSpec((1, 2, D_FF), lambda i: (i // tiles_per_seq, 0, 0))],
        out_shape=[jax.ShapeDtypeStruct((m, D_FF), BF16),
                   jax.ShapeDtypeStruct((n_seq, 2, D_FF), F32)],
        scratch_shapes=[pltpu.VMEM((8, D_FF), F32)],
        compiler_params=_params("arbitrary"),
        name="ffn_up_seq",
    )(x, g, w_ug, cw, cb)


def _ffn_up_step_kernel(x_ref, g_ref, wu_ref, wg_ref, cw_ref, cb_ref, b0_ref, b1_ref,
                        a_ref, n0_ref, n1_ref, h_ref):
    @pl.when(pl.program_id(0) == 0)
    def _():
        h_ref[...] = _rms(x_ref[...], g_ref[...]).astype(BF16)

    h = h_ref[...]
    u = jnp.dot(h, wu_ref[...], preferred_element_type=F32)
    gate = jnp.dot(h, wg_ref[...], preferred_element_type=F32)
    b1 = b1_ref[...]
    conv = cb_ref[...] + b0_ref[...] * cw_ref[0:1, :] + b1 * cw_ref[1:2, :] + gate * cw_ref[2:3, :]
    a_ref[...] = (conv * _sigmoid(conv) * u).astype(a_ref.dtype)
    n0_ref[...] = b1
    n1_ref[...] = gate


def ffn_up_step(x, g, w_ug, cw, cb, buf, tn):
    n_req, d = x.shape
    nt = D_FF // tn
    col = pl.BlockSpec((n_req, tn), lambda j: (0, j))
    return pl.pallas_call(
        _ffn_up_step_kernel,
        grid=(nt,),
        in_specs=[pl.BlockSpec((n_req, d), lambda j: (0, 0)),
                  pl.BlockSpec((1, d), lambda j: (0, 0)),
                  pl.BlockSpec((d, tn), lambda j: (0, j)),
                  pl.BlockSpec((d, tn), lambda j: (0, nt + j)),
                  pl.BlockSpec((3, tn), lambda j: (0, j)),
                  pl.BlockSpec((1, tn), lambda j: (0, j)),
                  col,
                  pl.BlockSpec((n_req, tn), lambda j: (0, nt + j))],
        out_specs=[col, col, col],
        out_shape=[jax.ShapeDtypeStruct((n_req, D_FF), BF16),
                   jax.ShapeDtypeStruct((n_req, D_FF), F32),
                   jax.ShapeDtypeStruct((n_req, D_FF), F32)],
        scratch_shapes=[pltpu.VMEM((n_req, d), BF16)],
        compiler_params=_params("arbitrary"),
        name="ffn_up_step",
    )(x, g, w_ug, w_ug, cw, cb, buf, buf)


def _pair_ones():
    r = lax.broadcasted_iota(jnp.int32, (LANES, LANES), 0) // RWKV_HEAD
    c = lax.broadcasted_iota(jnp.int32, (LANES, LANES), 1) // RWKV_HEAD
    return jnp.where(r == c, 1.0, 0.0).astype(BF16)


def _head_sum(x, ones, terms):
    total = None
    for _ in range(terms):
        piece = x.astype(BF16)
        part = jnp.dot(piece, ones, preferred_element_type=F32)
        total = part if total is None else total + part
        x = x - piece.astype(F32)
    return total


def _rwkv_proj_body(h, hprev, mu_ref, wrkv_ref, w1_ref, w2_ref, a1_ref, a2_ref, g1_ref, g2_ref, vec_ref,
                    vres, outs, mix_dtype):
    r_ref, k_ref, v_ref, kk_ref, ka_ref, ld_ref, g_ref = outs
    hm, dxm = h.astype(mix_dtype), (hprev - h).astype(mix_dtype)
    mix = lambda i: (hm + dxm * mu_ref[i:i + 1, :].astype(mix_dtype)).astype(BF16)
    xr, xw, xk, xv, xa, xg = (mix(i) for i in range(6))
    dot = lambda a, b: jnp.dot(a, b, preferred_element_type=F32)
    w0, a0, k_k, k_a = (vec_ref[i:i + 1, :] for i in range(4))
    r = dot(xr, wrkv_ref[0])
    k = dot(xk, wrkv_ref[1])
    v = dot(xv, wrkv_ref[2])
    z = w0 + dot(jnp.tanh(dot(xw, w1_ref[...])).astype(BF16), w2_ref[...])
    ld_ref[...] = -math.exp(-0.5) * _sigmoid(z)
    a = _sigmoid(a0 + dot(dot(xa, a1_ref[...]).astype(BF16), a2_ref[...]))
    g_ref[...] = dot(_sigmoid(dot(xg, g1_ref[...])).astype(BF16), g2_ref[...]).astype(g_ref.dtype)
    if vres is not None:
        vf_ref, v1_ref, v2_ref = vres
        v0 = vec_ref[4:5, :]
        v = v + (vf_ref[...] - v) * _sigmoid(v0 + dot(dot(xv, v1_ref[...]).astype(BF16), v2_ref[...]))
    kkr = k * k_k
    ones = _pair_ones()
    kk = jnp.concatenate(
        [kkr[:, t:t + LANES] * lax.rsqrt(jnp.maximum(
            _head_sum(kkr[:, t:t + LANES] * kkr[:, t:t + LANES], ones, 2), 1e-12))
         for t in range(0, D_MODEL, LANES)], axis=1)
    r_ref[...] = r.astype(r_ref.dtype)
    k_ref[...] = (k * (1.0 + (a - 1.0) * k_a)).astype(k_ref.dtype)
    v_ref[...] = v.astype(v_ref.dtype)
    kk_ref[...] = kk.astype(kk_ref.dtype)
    ka_ref[...] = (kk * a).astype(ka_ref.dtype)


def _rwkv_proj_seq_kernel(*refs, tiles_per_seq, has_vres):
    x_ref, gn_ref = refs[0], refs[1]
    weights = refs[2:11]
    n_in = 11 + (3 if has_vres else 0)
    vres = refs[11:14] if has_vres else None
    outs = refs[n_in:n_in + 7]
    shift_ref, carry_ref = refs[n_in + 7], refs[n_in + 8]
    m = pl.program_id(0)

    @pl.when(m % tiles_per_seq == 0)
    def _():
        carry_ref[...] = jnp.zeros_like(carry_ref)

    h = _rms(x_ref[...], gn_ref[...])
    tm = h.shape[0]
    rows = lax.broadcasted_iota(jnp.int32, (tm, 1), 0)
    hprev = jnp.where(rows == 0, carry_ref[0:1, :], pltpu.roll(h, 1, axis=0))
    _rwkv_proj_body(h, hprev, *weights, vres, outs, BF16)
    carry_ref[0:1, :] = h[tm - 1:tm, :]
    shift_ref[0] = h[tm - 1:tm, :]


def _rwkv_proj_step_kernel(*refs, has_vres):
    x_ref, gn_ref, prev_ref = refs[0], refs[1], refs[2]
    weights = refs[3:12]
    n_in = 12 + (3 if has_vres else 0)
    vres = refs[12:15] if has_vres else None
    outs = refs[n_in:n_in + 7]
    shift_ref = refs[n_in + 7]
    h = _rms(x_ref[...], gn_ref[...])
    _rwkv_proj_body(h, prev_ref[...], *weights, vres, outs, F32)
    shift_ref[...] = h


def rwkv_proj(x, gn, prev, wts, vres, tm, n_seq, tiles_per_seq):
    m, d = x.shape
    full = lambda a: pl.BlockSpec(a.shape, lambda i, _n=a.ndim: (0,) * _n)
    rowblk = pl.BlockSpec((tm, d), lambda i: (i, 0))
    seq = prev is None
    ins = [x, gn] + ([] if seq else [prev]) + list(wts)
    specs = [rowblk, full(gn)] + ([] if seq else [rowblk]) + [full(a) for a in wts]
    if vres is not None:
        vf, v1, v2 = vres
        ins += [vf, v1, v2]
        specs += [rowblk, full(v1), full(v2)]
    out_shape = [jax.ShapeDtypeStruct((m, d), F32)] * 7
    out_specs = [rowblk] * 7
    if seq:
        out_shape.append(jax.ShapeDtypeStruct((n_seq, 1, d), F32))
        out_specs.append(pl.BlockSpec((1, 1, d), lambda i: (i // tiles_per_seq, 0, 0)))
        body = functools.partial(_rwkv_proj_seq_kernel, tiles_per_seq=tiles_per_seq, has_vres=vres is not None)
        scratch = [pltpu.VMEM((8, d), F32)]
    else:
        out_shape.append(jax.ShapeDtypeStruct((m, d), F32))
        out_specs.append(rowblk)
        body = functools.partial(_rwkv_proj_step_kernel, has_vres=vres is not None)
        scratch = []
    return pl.pallas_call(
        body,
        grid=(m // tm,),
        in_specs=specs,
        out_specs=out_specs,
        out_shape=out_shape,
        scratch_shapes=scratch,
        compiler_params=_params("arbitrary"),
        name="rwkv_proj_seq" if seq else "rwkv_proj_step",
    )(*ins)


def _wkv_epilogue(y, r, k, v, g, lnw, lnb, rk):
    inv_n = 1.0 / RWKV_HEAD
    first = lax.broadcasted_iota(jnp.int32, (1, LANES), 1) < RWKV_HEAD

    def head_sum(x):
        s0 = jnp.sum(jnp.where(first, x, 0.0), axis=-1, keepdims=True)
        s1 = jnp.sum(jnp.where(first, 0.0, x), axis=-1, keepdims=True)
        return jnp.where(first, s0, s1)

    yc = y - head_sum(y) * inv_n
    yn = yc * lax.rsqrt(head_sum(yc * yc) * inv_n + RWKV_GN_EPS)
    return (yn * lnw + lnb + head_sum(r * k * rk) * v) * g


def _stack_heads(x):
    first = lax.broadcasted_iota(jnp.int32, (1, LANES), 1) < RWKV_HEAD
    return jnp.concatenate([jnp.where(first, x, 0.0), jnp.where(first, 0.0, x)], axis=0)


def _wkv_seq_kernel(*refs, pairs, has_acc, slab, n_slabs):
    r_ref, k_ref, v_ref, kk_ref, ka_ref, ld_ref, g_ref, lnw_ref, lnb_ref, rk_ref = refs[:10]
    z_ref, s_out_ref, s_ref = refs[10 + has_acc:]
    c = pl.program_id(1)
    L = WKV_CHUNK
    R = 2 * L

    @pl.when(c == 0)
    def _():
        s_ref[...] = jnp.zeros_like(s_ref)

    dot = lambda a, b: jnp.dot(a.astype(BF16), b.astype(BF16), preferred_element_type=F32)
    dot_nt = lambda a, b: lax.dot_general(a.astype(BF16), b.astype(BF16), NT_DIMS, preferred_element_type=F32)
    dot_tn = lambda a, b: lax.dot_general(a.astype(BF16), b.astype(BF16), TN_DIMS, preferred_element_type=F32)
    ti = lax.broadcasted_iota(jnp.int32, (L, L), 0)
    tj = lax.broadcasted_iota(jnp.int32, (L, L), 1)
    tri = jnp.where(ti >= tj, 1.0, 0.0).astype(BF16)
    ri = lax.broadcasted_iota(jnp.int32, (R, R), 0)
    rj = lax.broadcasted_iota(jnp.int32, (R, R), 1)
    same = (ri // L) == (rj // L)
    lower = same & ((ri % L) > (rj % L))
    lower_eq = same & ((ri % L) >= (rj % L))
    eye = jnp.where(ri == rj, 1.0, 0.0)

    for first in range(0, pairs, WKV_GROUP):
        _wkv_group(range(first, first + WKV_GROUP), refs[:10], z_ref, s_out_ref, s_ref, c,
                   (dot, dot_nt, dot_tn, tri, lower, lower_eq, eye))

    @pl.when(c == pl.num_programs(1) - 1)
    def _():
        for p in range(pairs):
            s_out_ref[slab, 0, 2 * p] = s_ref[p, 0:RWKV_HEAD, 0:RWKV_HEAD]
            s_out_ref[slab, 0, 2 * p + 1] = s_ref[p, RWKV_HEAD:LANES, RWKV_HEAD:LANES]
        _zero_other_slabs(s_out_ref, slab, n_slabs)


def _wkv_group(P, ins, z_ref, s_out_ref, s_ref, c, consts):
    r_ref, k_ref, v_ref, kk_ref, ka_ref, ld_ref, g_ref, lnw_ref, lnb_ref, rk_ref = ins
    dot, dot_nt, dot_tn, tri, lower, lower_eq, eye = consts
    L = WKV_CHUNK
    R = 2 * L
    P = list(P)
    sls = {p: slice(p * LANES, (p + 1) * LANES) for p in P}
    ld = {p: ld_ref[:, sls[p]] for p in P}
    ld_hi = {p: ld[p].astype(BF16) for p in P}
    ld_mid = {p: (ld[p] - ld_hi[p].astype(F32)).astype(BF16) for p in P}
    ld_lo = {p: (ld[p] - ld_hi[p].astype(F32) - ld_mid[p].astype(F32)).astype(BF16) for p in P}
    cum = {p: dot(tri, ld_hi[p]) + dot(tri, ld_mid[p]) + dot(tri, ld_lo[p]) for p in P}
    tot = {p: cum[p][L - 1:L, :] for p in P}
    ar, bk, bk_end, v_s = {}, {}, {}, {}
    for p in P:
        r, k, v, kk, ka = (ref[:, sls[p]] for ref in (r_ref, k_ref, v_ref, kk_ref, ka_ref))
        dec_out = jnp.exp(-cum[p])
        dec_end = jnp.exp(tot[p] - cum[p])
        ar[p] = jnp.concatenate([_stack_heads(jnp.exp(cum[p] - ld[p]) * (-kk)),
                                 _stack_heads(jnp.exp(cum[p]) * r)], axis=0).astype(BF16)
        bk[p] = jnp.concatenate([_stack_heads(dec_out * ka), _stack_heads(dec_out * k)], axis=0).astype(BF16)
        bk_end[p] = jnp.concatenate([_stack_heads(dec_end * ka), _stack_heads(dec_end * k)], axis=0).astype(BF16)
        v_s[p] = _stack_heads(v)
    big = {p: dot_nt(ar[p], bk[p]) for p in P}
    n_ab = {p: jnp.where(lower, big[p][0:R, 0:R], 0.0) for p in P}
    a_ak = {p: jnp.where(lower, big[p][0:R, R:2 * R], 0.0).astype(BF16) for p in P}
    a_r = {p: jnp.concatenate([jnp.where(lower_eq, big[p][R:2 * R, 0:R], 0.0),
                               jnp.where(lower_eq, big[p][R:2 * R, R:2 * R], 0.0)], axis=1).astype(BF16)
           for p in P}
    inv = {p: eye + n_ab[p] for p in P}
    pw = {p: dot(n_ab[p], n_ab[p]) for p in P}
    for _ in range(int(math.log2(L)) - 2):
        both = {p: dot(jnp.concatenate([pw[p], inv[p]], axis=0), pw[p]) for p in P}
        inv = {p: inv[p] + both[p][R:2 * R] for p in P}
        pw = {p: both[p][0:R] for p in P}
    inv = {p: inv[p] + dot(inv[p], pw[p]) for p in P}
    ars = {p: dot_nt(ar[p], s_ref[p]) for p in P}
    akv = {p: dot(a_ak[p], v_s[p]) for p in P}
    u_s = {p: dot(inv[p], ars[p][0:R] + akv[p]) for p in P}
    uv = {p: jnp.concatenate([u_s[p], v_s[p]], axis=0).astype(BF16) for p in P}
    y_s = {p: ars[p][R:2 * R] + dot(a_r[p], uv[p]) for p in P}
    for p in P:
        s_ref[p] = s_ref[p] * jnp.exp(tot[p]) + dot_tn(uv[p], bk_end[p])
    for p in P:
        sl = sls[p]
        y = y_s[p][0:L, :] + y_s[p][L:R, :]
        z_ref[:, sl] = _wkv_epilogue(y, r_ref[:, sl], k_ref[:, sl], v_ref[:, sl], g_ref[:, sl], lnw_ref[:, sl],
                                     lnb_ref[:, sl], rk_ref[:, sl]).astype(z_ref.dtype)


def wkv_seq(r, k, v, kk, ka, ld, g, lnw, lnb, rk, n_seq, chunks, layer, n_layers, acc):
    m, d = r.shape
    L = WKV_CHUNK
    blk = pl.BlockSpec((L, d), lambda b, c: (b * chunks + c, 0))
    vec = pl.BlockSpec((1, d), lambda b, c: (0, 0))
    ins = [r, k, v, kk, ka, ld, g, lnw, lnb, rk]
    specs = [blk] * 7 + [vec] * 3
    aliases = _layer_slab(acc, ins, specs, 1)
    st, slab, n_slabs = _slab_spec(acc, layer, n_layers, (1, RWKV_HEADS, RWKV_HEAD, RWKV_HEAD),
                                   lambda b, c: (b, 0, 0, 0))
    return pl.pallas_call(
        functools.partial(_wkv_seq_kernel, pairs=d // LANES, has_acc=acc is not None, slab=slab, n_slabs=n_slabs),
        grid=(n_seq, chunks),
        in_specs=specs,
        out_specs=[blk, st],
        out_shape=[jax.ShapeDtypeStruct((m, d), BF16),
                   jax.ShapeDtypeStruct((n_layers, n_seq, RWKV_HEADS, RWKV_HEAD, RWKV_HEAD), F32)],
        scratch_shapes=[pltpu.VMEM((d // LANES, LANES, LANES), F32)],
        input_output_aliases=aliases,
        compiler_params=_params("parallel", "arbitrary"),
        name="wkv_seq",
    )(*ins)


def _wkv_step_kernel(*refs, has_acc, slab, n_slabs):
    r_ref, k_ref, v_ref, kk_ref, ka_ref, ld_ref, g_ref, lnw_ref, lnb_ref, rk_ref, s_ref = refs[:11]
    z_ref, s_out_ref, vt_ref, y_ref = refs[11 + has_acc:]
    _zero_other_slabs(s_out_ref, slab, n_slabs)
    N = RWKV_HEAD
    SUB = 8
    r, k, v, nkk, ka, g = (ref[...].T for ref in (r_ref, k_ref, v_ref, kk_ref, ka_ref, g_ref))
    nkk = -nkk
    w = jnp.exp(ld_ref[...].T)
    vt_ref[...] = v
    sub = lax.broadcasted_iota(jnp.int32, (SUB, 1), 0)
    for hh in range(2):
        rows = slice(hh * N, (hh + 1) * N)
        r_h, k_h, nkk_h, ka_h, w_h = r[rows], k[rows], nkk[rows], ka[rows], w[rows]

        def value_rows(blk, carry, hh=hh, r_h=r_h, k_h=k_h, nkk_h=nkk_h, ka_h=ka_h, w_h=w_h):
            base = pl.multiple_of(hh * N + blk * SUB, SUB)
            v_blk = vt_ref[pl.ds(base, SUB), :]
            y_blk = jnp.zeros((SUB, v_blk.shape[1]), F32)
            for ii in range(SUB):
                s_old = s_ref[0, hh, blk * SUB + ii]
                sa = jnp.sum(s_old * nkk_h, axis=0, keepdims=True)
                s_new = s_old * w_h + sa * ka_h + v_blk[ii:ii + 1, :] * k_h
                s_out_ref[slab, hh, blk * SUB + ii] = s_new
                y_blk = jnp.where(sub == ii, jnp.sum(s_new * r_h, axis=0, keepdims=True), y_blk)
            y_ref[pl.ds(base, SUB), :] = y_blk
            return carry

        lax.fori_loop(0, N // SUB, value_rows, 0)

    def per_head(x, op):
        return jnp.concatenate([jnp.broadcast_to(op(x[hh * N:(hh + 1) * N], axis=0, keepdims=True), (N, x.shape[1]))
                                for hh in range(2)], axis=0)

    y = y_ref[...]
    yc = y - per_head(y, jnp.mean)
    yn = yc * lax.rsqrt(per_head(yc * yc, jnp.mean) + RWKV_GN_EPS)
    bonus = per_head(r * k * rk_ref[...], jnp.sum) * v
    z_ref[...] = ((yn * lnw_ref[...] + lnb_ref[...] + bonus) * g).T


def wkv_step(r, k, v, kk, ka, ld, g, lnw, lnb, rk, states_t, layer, acc):
    n, d = r.shape
    tile = pl.BlockSpec((n, LANES), lambda p: (0, p))
    col = pl.BlockSpec((LANES, 1), lambda p: (p, 0))
    st = pl.BlockSpec((1, 2) + states_t.shape[2:], lambda p: (layer, p, 0, 0, 0))
    ins = [r, k, v, kk, ka, ld, g, lnw, lnb, rk, states_t]
    specs = [tile] * 7 + [col] * 3 + [st]
    aliases = _layer_slab(acc, ins, specs, 1)
    st_out, slab, n_slabs = _slab_spec(acc, layer, states_t.shape[0], (2,) + states_t.shape[2:],
                                       lambda p: (p, 0, 0, 0))
    return pl.pallas_call(
        functools.partial(_wkv_step_kernel, has_acc=acc is not None, slab=slab, n_slabs=n_slabs),
        grid=(d // LANES,),
        in_specs=specs,
        out_specs=[tile, st_out],
        out_shape=[jax.ShapeDtypeStruct((n, d), F32), jax.ShapeDtypeStruct(states_t.shape, F32)],
        scratch_shapes=[pltpu.VMEM((LANES, n), F32), pltpu.VMEM((LANES, n), F32)],
        input_output_aliases=aliases,
        compiler_params=_params("parallel"),
        name="wkv_step",
    )(*ins)


def _trunk(x, seq, states, p):
    depth = p['norm_mix'].shape[0]
    rows = x.shape[0]
    if seq is not None:
        n_seq, tp, lead = seq
        tm = _row_tile(tp, 1088)
        tm_small = _row_tile(tp, 544)
        tm_rwkv = _row_tile(tp, 272)
        tiles_small, tiles_rwkv = tp // tm_small, tp // tm_rwkv
        cos, sin = rot_table(tp, -lead, 1)
    else:
        ret_s, wkv_s, shift_s, conv_s = states
        wkv_s = jnp.transpose(wkv_s, (0, 2, 3, 4, 1))
        tm = tm_small = rows
        tiles_small = 1
        lead = 0
        cos, sin = rot_table(8, PAST_LEN, 0)
    new_ret = new_wkv = None
    new_shift, new_conv = [], []
    n_ret, n_rwkv = (depth + 1) // 2, depth // 2
    v_first = None
    for i in range(depth):
        j = i // 2
        gn = p['norm_mix'][i][None]
        if i % 2 == 0:
            gnw = p['ret_gn_w'][j][None]
            if seq is not None:
                proj = ret_proj(x, gn, p['ret_w_in'][j], cos, sin, tm, tp // tm)
                o, new_ret = retention_prompt(proj, gnw, n_seq, tp // RET_CHUNK, j, n_ret, new_ret)
            else:
                proj = norm_proj(x, gn, p['ret_w_in'][j], tm, 1024, F32)
                o, new_ret = retention_step(proj[:, None, :], cos, sin, gnw, ret_s, j, new_ret)
                o = o[:, 0, :]
            x = matmul_res(o, p['ret_w_out'][j], x, tm_small, tiles_small, lead)
        else:
            vecs = [p['rwkv_w0'][j], p['rwkv_a0'][j], p['rwkv_k_k'][j], p['rwkv_k_a'][j]]
            vecs.append(p['rwkv_v0'][j - 1] if j else jnp.zeros_like(vecs[0]))
            vecs = jnp.stack(vecs + [jnp.zeros_like(vecs[0])] * 3)
            wts = [p['rwkv_mu'][j], p['rwkv_w_rkv'][j], p['rwkv_w1'][j], p['rwkv_w2'][j], p['rwkv_a1'][j],
                   p['rwkv_a2'][j], p['rwkv_g1'][j], p['rwkv_g2'][j], vecs]
            vres = (v_first, p['rwkv_v1'][j - 1], p['rwkv_v2'][j - 1]) if j else None
            lnw, lnb = p['rwkv_ln_w'][j][None], p['rwkv_ln_b'][j][None]
            rk = p['rwkv_r_k'][j].reshape(1, D_MODEL)
            if seq is not None:
                r, k, v, kk, ka, ld, g, sh = rwkv_proj(x, gn, None, wts, vres, tm_rwkv, n_seq, tiles_rwkv)
                z, new_wkv = wkv_seq(r, k, v, kk, ka, ld, g, lnw, lnb, rk, n_seq, tp // WKV_CHUNK,
                                     j, n_rwkv, new_wkv)
                sh = sh[:, 0, :]
            else:
                r, k, v, kk, ka, ld, g, sh = rwkv_proj(x, gn, shift_s[j], wts, vres, tm, 1, 1)
                z, new_wkv = wkv_step(r, k, v, kk, ka, ld, g, *(t.reshape(D_MODEL, 1) for t in (lnw, lnb, rk)),
                                      wkv_s, j, new_wkv)
            if v_first is None:
                v_first = v
            new_shift.append(sh)
            x = matmul_res(z, p['rwkv_w_o'][j], x, tm_small, tiles_small, lead)
        gf = p['norm_ffn'][i][None]
        cw, cb = p['ffn_conv_w'][i], p['ffn_conv_b'][i][None]
        if seq is not None:
            a, cbuf = ffn_up_seq(x, gf, p['ffn_w_ug'][i], cw, cb, tm_small, n_seq, tiles_small)
        else:
            a, n0, n1 = ffn_up_step(x, gf, p['ffn_w_ug'][i], cw, cb, conv_s[i].reshape(rows, 2 * D_FF), D_FF // 2)
            cbuf = jnp.stack([n0, n1], axis=1)
        new_conv.append(cbuf)
        x = matmul_res(a, p['ffn_w_d'][i], x, tm_small, tiles_small, lead)
    if seq is None:
        new_wkv = jnp.transpose(new_wkv, (0, 4, 1, 2, 3))
    return x, new_ret, new_wkv, jnp.stack(new_shift), jnp.stack(new_conv)


def kernel(x_prompt, x_sample, state_ret, state_wkv, state_shift, state_conv, meta_tokens, norm_mix, norm_ffn, norm_final, ret_w_in, ret_gn_w, ret_w_out, rwkv_mu, rwkv_w_rkv, rwkv_w0, rwkv_w1, rwkv_w2, rwkv_a0, rwkv_a1, rwkv_a2, rwkv_v0, rwkv_v1, rwkv_v2, rwkv_g1, rwkv_g2, rwkv_k_k, rwkv_k_a, rwkv_r_k, rwkv_ln_w, rwkv_ln_b, rwkv_w_o, ffn_w_ug, ffn_conv_w, ffn_conv_b, ffn_w_d):
    bf = lambda w: w.astype(BF16)
    p = dict(norm_mix=norm_mix, norm_ffn=norm_ffn, ret_w_in=bf(ret_w_in), ret_gn_w=ret_gn_w,
             ret_w_out=bf(ret_w_out), rwkv_mu=rwkv_mu, rwkv_w_rkv=bf(rwkv_w_rkv), rwkv_w0=rwkv_w0,
             rwkv_w1=bf(rwkv_w1), rwkv_w2=bf(rwkv_w2), rwkv_a0=rwkv_a0, rwkv_a1=bf(rwkv_a1),
             rwkv_a2=bf(rwkv_a2), rwkv_v0=rwkv_v0, rwkv_v1=bf(rwkv_v1), rwkv_v2=bf(rwkv_v2),
             rwkv_g1=bf(rwkv_g1), rwkv_g2=bf(rwkv_g2), rwkv_k_k=rwkv_k_k, rwkv_k_a=rwkv_k_a,
             rwkv_r_k=rwkv_r_k, rwkv_ln_w=rwkv_ln_w, rwkv_ln_b=rwkv_ln_b, rwkv_w_o=bf(rwkv_w_o),
             ffn_w_ug=bf(ffn_w_ug), ffn_conv_w=ffn_conv_w, ffn_conv_b=ffn_conv_b, ffn_w_d=bf(ffn_w_d))
    B, S, D = x_prompt.shape
    lead = (-N_META) % RET_CHUNK
    tp = lead + N_META + S
    assert tp % RET_CHUNK == 0 and D == D_MODEL
    meta = jnp.broadcast_to(meta_tokens[None].astype(F32), (B, N_META, D))
    xp = jnp.concatenate([jnp.zeros((B, lead, D), F32), meta, x_prompt.astype(F32)], axis=1).reshape(B * tp, D)
    xp, p_ret, p_wkv, p_shift, p_conv = _trunk(xp, (B, tp, lead), None, p)
    y_prompt = final_norm(xp.reshape(B, tp, D), norm_final[None], lead + N_META)

    n_req = x_sample.shape[0]
    xs, s_ret, s_wkv, s_shift, s_conv = _trunk(
        x_sample.reshape(n_req, D).astype(F32), None, (state_ret, state_wkv, state_shift, state_conv), p)
    y_sample = final_norm(xs[None], norm_final[None], 0).reshape(n_req, 1, D)
    return (y_prompt.astype(x_prompt.dtype), y_sample.astype(x_sample.dtype),
            p_ret, p_wkv, p_shift, p_conv, s_ret, s_wkv, s_shift, s_conv)
```

```python
import functools
import math

import jax
import jax.numpy as jnp
from jax import lax
from jax.experimental import pallas as pl
from jax.experimental.pallas import tpu as pltpu

F32 = jnp.float32
BF16 = jnp.bfloat16

D_MODEL = 1024
N_META = 16
PAST_LEN = 16384
RET_HEADS = 4
RET_DK = D_MODEL // RET_HEADS
RET_DV = 2 * RET_DK
RET_VDIM = RET_HEADS * RET_DV
RET_CHUNK = 128
RWKV_HEAD = 64
RWKV_HEADS = D_MODEL // RWKV_HEAD
D_FF = 2816
RMS_EPS = 1e-6
RET_GN_EPS = 1e-5
RWKV_GN_EPS = 64e-5

LANES = 128
WKV_CHUNK = 64
WKV_GROUP = 8
VMEM_LIMIT = 56 * 1024 * 1024

NT_DIMS = (((1,), (1,)), ((), ()))
TN_DIMS = (((0,), (0,)), ((), ()))


def _params(*sem):
    return pltpu.CompilerParams(dimension_semantics=sem, vmem_limit_bytes=VMEM_LIMIT)


def _rms(x, g):
    return x * lax.rsqrt(jnp.mean(x * x, axis=-1, keepdims=True) + RMS_EPS) * g


def _sigmoid(x):
    return 1.0 / (1.0 + jnp.exp(-x))


def _row_tile(rows_per_seq, cap):
    best = None
    for t in range(16, min(rows_per_seq, cap) + 1, 16):
        if rows_per_seq % t == 0:
            best = t
    assert best is not None, rows_per_seq
    return best


def _rot_table_kernel(inv_ref, cos_ref, sin_ref, *, pos0, step):
    rows = lax.broadcasted_iota(jnp.int32, cos_ref.shape, 0)
    pos = (rows * step + pos0).astype(F32)
    ang = pos * inv_ref[...]
    cos_ref[...] = jnp.cos(ang)
    sin_ref[...] = jnp.sin(ang)


def rot_table(n_rows, pos0, step):
    half = RET_DK // 2
    inv = (1.0 / (10000.0 ** jnp.linspace(0.0, 1.0, half, dtype=F32))).reshape(1, half)
    return pl.pallas_call(
        functools.partial(_rot_table_kernel, pos0=pos0, step=step),
        out_shape=(jax.ShapeDtypeStruct((n_rows, half), F32),) * 2,
        name="rot_table",
    )(inv)


def _norm_proj_kernel(x_ref, g_ref, w_ref, o_ref, h_ref):
    @pl.when(pl.program_id(1) == 0)
    def _():
        h_ref[...] = _rms(x_ref[...], g_ref[...]).astype(BF16)

    o_ref[...] = jnp.dot(h_ref[...], w_ref[...], preferred_element_type=F32).astype(o_ref.dtype)


def norm_proj(x, g, w, tm, tn, out_dtype):
    m, d = x.shape
    n = w.shape[1]
    return pl.pallas_call(
        _norm_proj_kernel,
        grid=(m // tm, n // tn),
        in_specs=[pl.BlockSpec((tm, d), lambda i, j: (i, 0)),
                  pl.BlockSpec((1, d), lambda i, j: (0, 0)),
                  pl.BlockSpec((d, tn), lambda i, j: (0, j))],
        out_specs=pl.BlockSpec((tm, tn), lambda i, j: (i, j)),
        out_shape=jax.ShapeDtypeStruct((m, n), out_dtype),
        scratch_shapes=[pltpu.VMEM((tm, d), BF16)],
        compiler_params=_params("parallel", "arbitrary"),
        name="norm_proj",
    )(x, g, w)


def _matmul_res_kernel(a_ref, w_ref, x_ref, o_ref, *, tiles_per_seq, lead):
    y = x_ref[...] + jnp.dot(a_ref[...].astype(BF16), w_ref[...], preferred_element_type=F32)
    if lead:
        tm = y.shape[0]
        row = (pl.program_id(0) % tiles_per_seq) * tm + lax.broadcasted_iota(jnp.int32, (tm, 1), 0)
        y = jnp.where(row >= lead, y, 0.0)
    o_ref[...] = y


def matmul_res(a, w, x, tm, tiles_per_seq, lead):
    m, k = a.shape
    n = w.shape[1]
    return pl.pallas_call(
        functools.partial(_matmul_res_kernel, tiles_per_seq=tiles_per_seq, lead=lead),
        grid=(m // tm,),
        in_specs=[pl.BlockSpec((tm, k), lambda i: (i, 0)),
                  pl.BlockSpec((k, n), lambda i: (0, 0), pipeline_mode=pl.Buffered(1)),
                  pl.BlockSpec((tm, n), lambda i: (i, 0))],
        out_specs=pl.BlockSpec((tm, n), lambda i: (i, 0)),
        out_shape=jax.ShapeDtypeStruct((m, n), F32),
        compiler_params=_params("parallel"),
        name="matmul_res",
    )(a, w, x)


def _final_norm_kernel(x_ref, g_ref, o_ref, *, skip):
    o_ref[0] = _rms(x_ref[0, skip:, :], g_ref[...])


def final_norm(x, g, skip):
    n_seq, rows, d = x.shape
    return pl.pallas_call(
        functools.partial(_final_norm_kernel, skip=skip),
        grid=(n_seq,),
        in_specs=[pl.BlockSpec((1, rows, d), lambda b: (b, 0, 0)),
                  pl.BlockSpec((1, d), lambda b: (0, 0))],
        out_specs=pl.BlockSpec((1, rows - skip, d), lambda b: (b, 0, 0)),
        out_shape=jax.ShapeDtypeStruct((n_seq, rows - skip, d), F32),
        compiler_params=_params("parallel"),
        name="final_norm",
    )(x, g)


def _rotate(x, cos, sin):
    half = RET_DK // 2
    x1, x2 = x[:, :half], x[:, half:]
    return jnp.concatenate([x1 * cos - x2 * sin, x1 * sin + x2 * cos], axis=1)


def _gated_head_norm(o, silu_g, gnw):
    oc = o - jnp.mean(o, axis=-1, keepdims=True)
    on = oc * lax.rsqrt(jnp.mean(oc * oc, axis=-1, keepdims=True) + RET_GN_EPS)
    return silu_g * (on * gnw)


def _log_g(h):
    return math.log(1.0 - 2.0 ** (-5.0 - h))


def _layer_slab(acc, ins, specs, out_index):
    if acc is None:
        return {}
    ins.append(acc)
    specs.append(pl.BlockSpec(memory_space=pl.ANY))
    return {len(ins) - 1: out_index}


def _slab_spec(acc, layer, n_layers, rest_block, rest_index):
    if acc is None:
        return pl.BlockSpec((n_layers,) + rest_block, lambda *g: (0,) + rest_index(*g)), layer, n_layers
    return pl.BlockSpec((1,) + rest_block, lambda *g: (layer,) + rest_index(*g)), 0, 1


def _zero_other_slabs(s_out_ref, slab, n_slabs):
    for other in range(n_slabs):
        if other != slab:
            s_out_ref[other] = jnp.zeros(s_out_ref.shape[1:], s_out_ref.dtype)


def _retention_kernel(*refs, has_acc, slab, n_slabs):
    p_ref, cos_ref, sin_ref, gnw_ref = refs[:4]
    o_ref, s_out_ref, s_ref = refs[4 + has_acc:]
    c = pl.program_id(1)
    L = RET_CHUNK
    heads = range(RET_HEADS)
    qk = RET_HEADS * RET_DK

    @pl.when(c == 0)
    def _():
        s_ref[...] = jnp.zeros_like(s_ref)

    dot = lambda a, b: jnp.dot(a, b, preferred_element_type=F32)
    ri = lax.broadcasted_iota(jnp.int32, (L, 1), 0).astype(F32)
    ci = lax.broadcasted_iota(jnp.int32, (1, L), 1).astype(F32)
    rel = ri - ci
    lg = [_log_g(h) for h in heads]
    cos, sin = cos_ref[...], sin_ref[...]
    q = [_rotate(p_ref[:, h * RET_DK:(h + 1) * RET_DK].astype(F32), cos, sin).astype(BF16) for h in heads]
    kf = [_rotate(p_ref[:, qk + h * RET_DK:qk + (h + 1) * RET_DK].astype(F32), cos, sin) * (RET_DK ** -0.5)
          for h in heads]
    v = [p_ref[:, 2 * qk + h * RET_DV:2 * qk + (h + 1) * RET_DV] for h in heads]
    scores = [lax.dot_general(q[h], kf[h].astype(BF16), NT_DIMS, preferred_element_type=F32)
              * jnp.where(rel >= 0, jnp.exp(lg[h] * jnp.maximum(rel, 0.0)), 0.0) for h in heads]
    s_old = [s_ref[h] for h in heads]
    qs = [dot(q[h], s_old[h].astype(BF16)) for h in heads]
    o = [dot(scores[h].astype(BF16), v[h]) + jnp.exp(lg[h] * (ri + 1.0)) * qs[h] for h in heads]
    kd = [(kf[h] * jnp.exp(lg[h] * (L - 1.0 - ri))).astype(BF16) for h in heads]
    s_new = [math.exp(lg[h] * L) * s_old[h] + lax.dot_general(kd[h], v[h], TN_DIMS, preferred_element_type=F32)
             for h in heads]
    for h in heads:
        s_ref[h] = s_new[h]
    g0 = 2 * qk + RET_VDIM
    for h in heads:
        sl = slice(h * RET_DV, (h + 1) * RET_DV)
        g = p_ref[:, g0 + h * RET_DV:g0 + (h + 1) * RET_DV].astype(F32)
        o_ref[:, sl] = _gated_head_norm(o[h], g * _sigmoid(g), gnw_ref[:, sl]).astype(o_ref.dtype)

    @pl.when(c == pl.num_programs(1) - 1)
    def _():
        s_out_ref[slab, 0] = s_ref[...]
        _zero_other_slabs(s_out_ref, slab, n_slabs)


def retention_prompt(proj, cos, sin, gnw, n_seq, chunks, layer, n_layers, acc):
    L = RET_CHUNK
    m, width = proj.shape
    ins = [proj, cos, sin, gnw]
    specs = [pl.BlockSpec((L, width), lambda b, c: (b * chunks + c, 0)),
             pl.BlockSpec((L, RET_DK // 2), lambda b, c: (c, 0)),
             pl.BlockSpec((L, RET_DK // 2), lambda b, c: (c, 0)),
             pl.BlockSpec((1, RET_VDIM), lambda b, c: (0, 0))]
    aliases = _layer_slab(acc, ins, specs, 1)
    st, slab, n_slabs = _slab_spec(acc, layer, n_layers, (1, RET_HEADS, RET_DK, RET_DV), lambda b, c: (b, 0, 0, 0))
    return pl.pallas_call(
        functools.partial(_retention_kernel, has_acc=acc is not None, slab=slab, n_slabs=n_slabs),
        grid=(n_seq, chunks),
        in_specs=specs,
        out_specs=[pl.BlockSpec((L, RET_VDIM), lambda b, c: (b * chunks + c, 0)), st],
        out_shape=[jax.ShapeDtypeStruct((m, RET_VDIM), BF16),
                   jax.ShapeDtypeStruct((n_layers, n_seq, RET_HEADS, RET_DK, RET_DV), F32)],
        scratch_shapes=[pltpu.VMEM((RET_HEADS, RET_DK, RET_DV), F32)],
        input_output_aliases=aliases,
        compiler_params=_params("parallel", "arbitrary"),
        name="retention_prompt",
    )(*ins)


def _to_column(row, n):
    eye = lax.broadcasted_iota(jnp.int32, (n, n), 0) == lax.broadcasted_iota(jnp.int32, (n, n), 1)
    return jnp.sum(jnp.where(eye, jnp.broadcast_to(row, (n, n)), 0.0), axis=-1, keepdims=True)


def _retention_step_kernel(*refs, has_acc, slab, n_slabs):
    p_ref, cos_ref, sin_ref, gnw_ref, s_ref = refs[:5]
    o_ref, s_out_ref = refs[5 + has_acc:]
    _zero_other_slabs(s_out_ref, slab, n_slabs)
    cos, sin = cos_ref[0:1, :], sin_ref[0:1, :]
    qk = RET_HEADS * RET_DK
    for h in range(RET_HEADS):
        decay = math.exp(_log_g(h))
        q = _rotate(p_ref[0, :, h * RET_DK:(h + 1) * RET_DK], cos, sin)
        k = _rotate(p_ref[0, :, qk + h * RET_DK:qk + (h + 1) * RET_DK], cos, sin) * (RET_DK ** -0.5)
        v = p_ref[0, :, 2 * qk + h * RET_DV:2 * qk + (h + 1) * RET_DV]
        g = p_ref[0, :, 2 * qk + RET_VDIM + h * RET_DV:2 * qk + RET_VDIM + (h + 1) * RET_DV]
        s_old = s_ref[0, 0, h]
        qs = jnp.sum(s_old * _to_column(q, RET_DK), axis=0, keepdims=True)
        o = jnp.sum(q * k, axis=-1, keepdims=True) * v + decay * qs
        s_out_ref[slab, 0, h] = decay * s_old + _to_column(k, RET_DK) * v
        o_ref[0, :, h * RET_DV:(h + 1) * RET_DV] = _gated_head_norm(
            o, g * _sigmoid(g), gnw_ref[:, h * RET_DV:(h + 1) * RET_DV]).astype(o_ref.dtype)


def retention_step(proj, cos, sin, gnw, states, layer, acc):
    n = proj.shape[0]
    st = pl.BlockSpec((1, 1, RET_HEADS, RET_DK, RET_DV), lambda b: (layer, b, 0, 0, 0))
    ins = [proj, cos, sin, gnw, states]
    specs = [pl.BlockSpec((1, 1, proj.shape[2]), lambda b: (b, 0, 0)),
             pl.BlockSpec(cos.shape, lambda b: (0, 0)),
             pl.BlockSpec(sin.shape, lambda b: (0, 0)),
             pl.BlockSpec((1, RET_VDIM), lambda b: (0, 0)),
             st]
    aliases = _layer_slab(acc, ins, specs, 1)
    st_out, slab, n_slabs = _slab_spec(acc, layer, states.shape[0], (1, RET_HEADS, RET_DK, RET_DV),
                                       lambda b: (b, 0, 0, 0))
    return pl.pallas_call(
        functools.partial(_retention_step_kernel, has_acc=acc is not None, slab=slab, n_slabs=n_slabs),
        grid=(n,),
        in_specs=specs,
        out_specs=[pl.BlockSpec((1, 1, RET_VDIM), lambda b: (b, 0, 0)), st_out],
        out_shape=[jax.ShapeDtypeStruct((n, 1, RET_VDIM), BF16),
                   jax.ShapeDtypeStruct(states.shape, F32)],
        input_output_aliases=aliases,
        compiler_params=_params("parallel"),
        name="retention_step",
    )(*ins)


FFN_COLS = 256


def _conv_gate(gate, p1, p2, cw, cb):
    rows = lax.broadcasted_iota(jnp.int32, (gate.shape[0], 1), 0)
    g1 = jnp.where(rows == 0, p1, pltpu.roll(gate, 1, axis=0))
    g2 = jnp.where(rows == 0, p2, jnp.where(rows == 1, p1, pltpu.roll(gate, 2, axis=0)))
    return cb + g2 * cw[0:1, :] + g1 * cw[1:2, :] + gate * cw[2:3, :]


def _ffn_up_seq_kernel(x_ref, g_ref, w_ref, cw_ref, cb_ref, a_ref, nc_ref, carry_ref, *, tiles_per_seq):
    @pl.when(pl.program_id(0) % tiles_per_seq == 0)
    def _():
        carry_ref[...] = jnp.zeros_like(carry_ref)

    h = _rms(x_ref[...], g_ref[...]).astype(BF16)
    tm = h.shape[0]
    for j in range(D_FF // FFN_COLS):
        sl = slice(j * FFN_COLS, (j + 1) * FFN_COLS)
        u = jnp.dot(h, w_ref[:, sl], preferred_element_type=F32)
        gate = jnp.dot(h, w_ref[:, D_FF + j * FFN_COLS:D_FF + (j + 1) * FFN_COLS], preferred_element_type=F32)
        conv = _conv_gate(gate, carry_ref[1:2, sl], carry_ref[0:1, sl], cw_ref[:, sl], cb_ref[:, sl])
        a_ref[:, sl] = (conv * _sigmoid(conv) * u).astype(a_ref.dtype)
        last = gate[tm - 2:tm, :]
        carry_ref[0:2, sl] = last
        nc_ref[0, :, sl] = last


def ffn_up_seq(x, g, w_ug, cw, cb, tm, n_seq, tiles_per_seq):
    m, d = x.shape
    return pl.pallas_call(
        functools.partial(_ffn_up_seq_kernel, tiles_per_seq=tiles_per_seq),
        grid=(m // tm,),
        in_specs=[pl.BlockSpec((tm, d), lambda i: (i, 0)),
                  pl.BlockSpec((1, d), lambda i: (0, 0)),
                  pl.BlockSpec(w_ug.shape, lambda i: (0, 0), pipeline_mode=pl.Buffered(1)),
                  pl.BlockSpec(cw.shape, lambda i: (0, 0)),
                  pl.BlockSpec(cb.shape, lambda i: (0, 0))],
        out_specs=[pl.BlockSpec((tm, D_FF), lambda i: (i, 0)),
                   pl.BlockSpec((1, 2, D_FF), lambda i: (i // tiles_per_seq, 0, 0))],
        out_shape=[jax.ShapeDtypeStruct((m, D_FF), BF16),
                   jax.ShapeDtypeStruct((n_seq, 2, D_FF), F32)],
        scratch_shapes=[pltpu.VMEM((8, D_FF), F32)],
        compiler_params=_params("arbitrary"),
        name="ffn_up_seq",
    )(x, g, w_ug, cw, cb)


def _ffn_up_step_kernel(x_ref, g_ref, wu_ref, wg_ref, cw_ref, cb_ref, b0_ref, b1_ref,
                        a_ref, n0_ref, n1_ref, h_ref):
    @pl.when(pl.program_id(0) == 0)
    def _():
        h_ref[...] = _rms(x_ref[...], g_ref[...]).astype(BF16)

    h = h_ref[...]
    u = jnp.dot(h, wu_ref[...], preferred_element_type=F32)
    gate = jnp.dot(h, wg_ref[...], preferred_element_type=F32)
    b1 = b1_ref[...]
    conv = cb_ref[...] + b0_ref[...] * cw_ref[0:1, :] + b1 * cw_ref[1:2, :] + gate * cw_ref[2:3, :]
    a_ref[...] = (conv * _sigmoid(conv) * u).astype(a_ref.dtype)
    n0_ref[...] = b1
    n1_ref[...] = gate


def ffn_up_step(x, g, w_ug, cw, cb, buf, tn):
    n_req, d = x.shape
    nt = D_FF // tn
    col = pl.BlockSpec((n_req, tn), lambda j: (0, j))
    return pl.pallas_call(
        _ffn_up_step_kernel,
        grid=(nt,),
        in_specs=[pl.BlockSpec((n_req, d), lambda j: (0, 0)),
                  pl.BlockSpec((1, d), lambda j: (0, 0)),
                  pl.BlockSpec((d, tn), lambda j: (0, j)),
                  pl.BlockSpec((d, tn), lambda j: (0, nt + j)),
                  pl.BlockSpec((3, tn), lambda j: (0, j)),
                  pl.BlockSpec((1, tn), lambda j: (0, j)),
                  col,
                  pl.BlockSpec((n_req, tn), lambda j: (0, nt + j))],
        out_specs=[col, col, col],
        out_shape=[jax.ShapeDtypeStruct((n_req, D_FF), BF16),
                   jax.ShapeDtypeStruct((n_req, D_FF), F32),
                   jax.ShapeDtypeStruct((n_req, D_FF), F32)],
        scratch_shapes=[pltpu.VMEM((n_req, d), BF16)],
        compiler_params=_params("arbitrary"),
        name="ffn_up_step",
    )(x, g, w_ug, w_ug, cw, cb, buf, buf)


def _pair_ones():
    r = lax.broadcasted_iota(jnp.int32, (LANES, LANES), 0) // RWKV_HEAD
    c = lax.broadcasted_iota(jnp.int32, (LANES, LANES), 1) // RWKV_HEAD
    return jnp.where(r == c, 1.0, 0.0).astype(BF16)


def _head_sum(x, ones, terms):
    total = None
    for _ in range(terms):
        piece = x.astype(BF16)
        part = jnp.dot(piece, ones, preferred_element_type=F32)
        total = part if total is None else total + part
        x = x - piece.astype(F32)
    return total


def _rwkv_proj_body(h, hprev, mu_ref, wrkv_ref, w1_ref, w2_ref, a1_ref, a2_ref, g1_ref, g2_ref, vec_ref,
                    vres, outs, mix_dtype):
    r_ref, k_ref, v_ref, kk_ref, ka_ref, ld_ref, g_ref = outs
    hm, dxm = h.astype(mix_dtype), (hprev - h).astype(mix_dtype)
    mix = lambda i: (hm + dxm * mu_ref[i:i + 1, :].astype(mix_dtype)).astype(BF16)
    xr, xw, xk, xv, xa, xg = (mix(i) for i in range(6))
    dot = lambda a, b: jnp.dot(a, b, preferred_element_type=F32)
    w0, a0, k_k, k_a = (vec_ref[i:i + 1, :] for i in range(4))
    r = dot(xr, wrkv_ref[0])
    k = dot(xk, wrkv_ref[1])
    v = dot(xv, wrkv_ref[2])
    z = w0 + dot(jnp.tanh(dot(xw, w1_ref[...])).astype(BF16), w2_ref[...])
    ld_ref[...] = -math.exp(-0.5) * _sigmoid(z)
    a = _sigmoid(a0 + dot(dot(xa, a1_ref[...]).astype(BF16), a2_ref[...]))
    g_ref[...] = dot(_sigmoid(dot(xg, g1_ref[...])).astype(BF16), g2_ref[...]).astype(g_ref.dtype)
    if vres is not None:
        vf_ref, v1_ref, v2_ref = vres
        v0 = vec_ref[4:5, :]
        v = v + (vf_ref[...] - v) * _sigmoid(v0 + dot(dot(xv, v1_ref[...]).astype(BF16), v2_ref[...]))
    kkr = k * k_k
    ones = _pair_ones()
    kk = jnp.concatenate(
        [kkr[:, t:t + LANES] * lax.rsqrt(jnp.maximum(
            _head_sum(kkr[:, t:t + LANES] * kkr[:, t:t + LANES], ones, 2), 1e-12))
         for t in range(0, D_MODEL, LANES)], axis=1)
    r_ref[...] = r.astype(r_ref.dtype)
    k_ref[...] = (k * (1.0 + (a - 1.0) * k_a)).astype(k_ref.dtype)
    v_ref[...] = v.astype(v_ref.dtype)
    kk_ref[...] = kk.astype(kk_ref.dtype)
    ka_ref[...] = (kk * a).astype(ka_ref.dtype)


def _rwkv_proj_seq_kernel(*refs, tiles_per_seq, has_vres):
    x_ref, gn_ref = refs[0], refs[1]
    weights = refs[2:11]
    n_in = 11 + (3 if has_vres else 0)
    vres = refs[11:14] if has_vres else None
    outs = refs[n_in:n_in + 7]
    shift_ref, carry_ref = refs[n_in + 7], refs[n_in + 8]
    m = pl.program_id(0)

    @pl.when(m % tiles_per_seq == 0)
    def _():
        carry_ref[...] = jnp.zeros_like(carry_ref)

    h = _rms(x_ref[...], gn_ref[...])
    tm = h.shape[0]
    rows = lax.broadcasted_iota(jnp.int32, (tm, 1), 0)
    hprev = jnp.where(rows == 0, carry_ref[0:1, :], pltpu.roll(h, 1, axis=0))
    _rwkv_proj_body(h, hprev, *weights, vres, outs, BF16)
    carry_ref[0:1, :] = h[tm - 1:tm, :]
    shift_ref[0] = h[tm - 1:tm, :]


def _rwkv_proj_step_kernel(*refs, has_vres):
    x_ref, gn_ref, prev_ref = refs[0], refs[1], refs[2]
    weights = refs[3:12]
    n_in = 12 + (3 if has_vres else 0)
    vres = refs[12:15] if has_vres else None
    outs = refs[n_in:n_in + 7]
    shift_ref = refs[n_in + 7]
    h = _rms(x_ref[...], gn_ref[...])
    _rwkv_proj_body(h, prev_ref[...], *weights, vres, outs, F32)
    shift_ref[...] = h


def rwkv_proj(x, gn, prev, wts, vres, tm, n_seq, tiles_per_seq):
    m, d = x.shape
    full = lambda a: pl.BlockSpec(a.shape, lambda i, _n=a.ndim: (0,) * _n, pipeline_mode=pl.Buffered(1))
    rowblk = pl.BlockSpec((tm, d), lambda i: (i, 0))
    seq = prev is None
    ins = [x, gn] + ([] if seq else [prev]) + list(wts)
    specs = [rowblk, full(gn)] + ([] if seq else [rowblk]) + [full(a) for a in wts]
    if vres is not None:
        vf, v1, v2 = vres
        ins += [vf, v1, v2]
        specs += [rowblk, full(v1), full(v2)]
    act = BF16 if seq else F32
    out_shape = [jax.ShapeDtypeStruct((m, d), dt) for dt in (act, F32, F32, F32, F32, F32, act)]
    out_specs = [rowblk] * 7
    if seq:
        out_shape.append(jax.ShapeDtypeStruct((n_seq, 1, d), F32))
        out_specs.append(pl.BlockSpec((1, 1, d), lambda i: (i // tiles_per_seq, 0, 0)))
        body = functools.partial(_rwkv_proj_seq_kernel, tiles_per_seq=tiles_per_seq, has_vres=vres is not None)
        scratch = [pltpu.VMEM((8, d), F32)]
    else:
        out_shape.append(jax.ShapeDtypeStruct((m, d), F32))
        out_specs.append(rowblk)
        body = functools.partial(_rwkv_proj_step_kernel, has_vres=vres is not None)
        scratch = []
    return pl.pallas_call(
        body,
        grid=(m // tm,),
        in_specs=specs,
        out_specs=out_specs,
        out_shape=out_shape,
        scratch_shapes=scratch,
        compiler_params=_params("arbitrary"),
        name="rwkv_proj_seq" if seq else "rwkv_proj_step",
    )(*ins)


def _wkv_epilogue(y, r, k, v, g, lnw, lnb, rk):
    inv_n = 1.0 / RWKV_HEAD
    first = lax.broadcasted_iota(jnp.int32, (1, LANES), 1) < RWKV_HEAD

    def head_sum(x):
        s0 = jnp.sum(jnp.where(first, x, 0.0), axis=-1, keepdims=True)
        s1 = jnp.sum(jnp.where(first, 0.0, x), axis=-1, keepdims=True)
        return jnp.where(first, s0, s1)

    yc = y - head_sum(y) * inv_n
    yn = yc * lax.rsqrt(head_sum(yc * yc) * inv_n + RWKV_GN_EPS)
    return (yn * lnw + lnb + head_sum(r * k * rk) * v) * g


def _stack_heads(x):
    first = lax.broadcasted_iota(jnp.int32, (1, LANES), 1) < RWKV_HEAD
    return jnp.concatenate([jnp.where(first, x, 0.0), jnp.where(first, 0.0, x)], axis=0)


def _wkv_seq_kernel(*refs, pairs, has_acc, slab, n_slabs):
    r_ref, k_ref, v_ref, kk_ref, ka_ref, ld_ref, g_ref, lnw_ref, lnb_ref, rk_ref = refs[:10]
    z_ref, s_out_ref, s_ref = refs[10 + has_acc:]
    c = pl.program_id(1)
    L = WKV_CHUNK
    R = 2 * L

    @pl.when(c == 0)
    def _():
        s_ref[...] = jnp.zeros_like(s_ref)

    dot = lambda a, b: jnp.dot(a.astype(BF16), b.astype(BF16), preferred_element_type=F32)
    dot_nt = lambda a, b: lax.dot_general(a.astype(BF16), b.astype(BF16), NT_DIMS, preferred_element_type=F32)
    dot_tn = lambda a, b: lax.dot_general(a.astype(BF16), b.astype(BF16), TN_DIMS, preferred_element_type=F32)
    ti = lax.broadcasted_iota(jnp.int32, (L, L), 0)
    tj = lax.broadcasted_iota(jnp.int32, (L, L), 1)
    tri = jnp.where(ti >= tj, 1.0, 0.0).astype(BF16)
    ri = lax.broadcasted_iota(jnp.int32, (R, R), 0)
    rj = lax.broadcasted_iota(jnp.int32, (R, R), 1)
    same = (ri // L) == (rj // L)
    lower = same & ((ri % L) > (rj % L))
    lower_eq = same & ((ri % L) >= (rj % L))
    eye = jnp.where(ri == rj, 1.0, 0.0)

    for first in range(0, pairs, WKV_GROUP):
        _wkv_group(range(first, first + WKV_GROUP), refs[:10], z_ref, s_out_ref, s_ref, c,
                   (dot, dot_nt, dot_tn, tri, lower, lower_eq, eye))

    @pl.when(c == pl.num_programs(1) - 1)
    def _():
        for p in range(pairs):
            s_out_ref[slab, 0, 2 * p] = s_ref[p, 0:RWKV_HEAD, 0:RWKV_HEAD]
            s_out_ref[slab, 0, 2 * p + 1] = s_ref[p, RWKV_HEAD:LANES, RWKV_HEAD:LANES]
        _zero_other_slabs(s_out_ref, slab, n_slabs)


def _wkv_group(P, ins, z_ref, s_out_ref, s_ref, c, consts):
    r_ref, k_ref, v_ref, kk_ref, ka_ref, ld_ref, g_ref, lnw_ref, lnb_ref, rk_ref = ins
    dot, dot_nt, dot_tn, tri, lower, lower_eq, eye = consts
    L = WKV_CHUNK
    R = 2 * L
    P = list(P)
    sls = {p: slice(p * LANES, (p + 1) * LANES) for p in P}
    ld = {p: ld_ref[:, sls[p]] for p in P}
    ld_hi = {p: ld[p].astype(BF16) for p in P}
    ld_mid = {p: (ld[p] - ld_hi[p].astype(F32)).astype(BF16) for p in P}
    ld_lo = {p: (ld[p] - ld_hi[p].astype(F32) - ld_mid[p].astype(F32)).astype(BF16) for p in P}
    cum = {p: dot(tri, ld_hi[p]) + dot(tri, ld_mid[p]) + dot(tri, ld_lo[p]) for p in P}
    tot = {p: cum[p][L - 1:L, :] for p in P}
    ar, bk, bk_end, v_s = {}, {}, {}, {}
    for p in P:
        r, k, v, kk, ka = (ref[:, sls[p]] for ref in (r_ref, k_ref, v_ref, kk_ref, ka_ref))
        dec_out = jnp.exp(-cum[p])
        dec_end = jnp.exp(tot[p] - cum[p])
        ar[p] = jnp.concatenate([_stack_heads(jnp.exp(cum[p] - ld[p]) * (-kk)),
                                 _stack_heads(jnp.exp(cum[p]) * r)], axis=0).astype(BF16)
        bk[p] = jnp.concatenate([_stack_heads(dec_out * ka), _stack_heads(dec_out * k)], axis=0).astype(BF16)
        bk_end[p] = jnp.concatenate([_stack_heads(dec_end * ka), _stack_heads(dec_end * k)], axis=0).astype(BF16)
        v_s[p] = _stack_heads(v)
    big = {p: dot_nt(ar[p], bk[p]) for p in P}
    n_ab = {p: jnp.where(lower, big[p][0:R, 0:R], 0.0) for p in P}
    a_ak = {p: jnp.where(lower, big[p][0:R, R:2 * R], 0.0).astype(BF16) for p in P}
    a_r = {p: jnp.concatenate([jnp.where(lower_eq, big[p][R:2 * R, 0:R], 0.0),
                               jnp.where(lower_eq, big[p][R:2 * R, R:2 * R], 0.0)], axis=1).astype(BF16)
           for p in P}
    inv = {p: eye + n_ab[p] for p in P}
    pw = {p: dot(n_ab[p], n_ab[p]) for p in P}
    for _ in range(int(math.log2(L)) - 2):
        both = {p: dot(jnp.concatenate([pw[p], inv[p]], axis=0), pw[p]) for p in P}
        inv = {p: inv[p] + both[p][R:2 * R] for p in P}
        pw = {p: both[p][0:R] for p in P}
    inv = {p: inv[p] + dot(inv[p], pw[p]) for p in P}
    ars = {p: dot_nt(ar[p], s_ref[p]) for p in P}
    akv = {p: dot(a_ak[p], v_s[p]) for p in P}
    u_s = {p: dot(inv[p], ars[p][0:R] + akv[p]) for p in P}
    uv = {p: jnp.concatenate([u_s[p], v_s[p]], axis=0).astype(BF16) for p in P}
    y_s = {p: ars[p][R:2 * R] + dot(a_r[p], uv[p]) for p in P}
    for p in P:
        s_ref[p] = s_ref[p] * jnp.exp(tot[p]) + dot_tn(uv[p], bk_end[p])
    for p in P:
        sl = sls[p]
        y = y_s[p][0:L, :] + y_s[p][L:R, :]
        z_ref[:, sl] = _wkv_epilogue(y, r_ref[:, sl], k_ref[:, sl], v_ref[:, sl], g_ref[:, sl], lnw_ref[:, sl],
                                     lnb_ref[:, sl], rk_ref[:, sl]).astype(z_ref.dtype)


def wkv_seq(r, k, v, kk, ka, ld, g, lnw, lnb, rk, n_seq, chunks, layer, n_layers, acc):
    m, d = r.shape
    L = WKV_CHUNK
    blk = pl.BlockSpec((L, d), lambda b, c: (b * chunks + c, 0))
    vec = pl.BlockSpec((1, d), lambda b, c: (0, 0))
    ins = [r, k, v, kk, ka, ld, g, lnw, lnb, rk]
    specs = [blk] * 7 + [vec] * 3
    aliases = _layer_slab(acc, ins, specs, 1)
    st, slab, n_slabs = _slab_spec(acc, layer, n_layers, (1, RWKV_HEADS, RWKV_HEAD, RWKV_HEAD),
                                   lambda b, c: (b, 0, 0, 0))
    return pl.pallas_call(
        functools.partial(_wkv_seq_kernel, pairs=d // LANES, has_acc=acc is not None, slab=slab, n_slabs=n_slabs),
        grid=(n_seq, chunks),
        in_specs=specs,
        out_specs=[blk, st],
        out_shape=[jax.ShapeDtypeStruct((m, d), BF16),
                   jax.ShapeDtypeStruct((n_layers, n_seq, RWKV_HEADS, RWKV_HEAD, RWKV_HEAD), F32)],
        scratch_shapes=[pltpu.VMEM((d // LANES, LANES, LANES), F32)],
        input_output_aliases=aliases,
        compiler_params=_params("parallel", "arbitrary"),
        name="wkv_seq",
    )(*ins)


def _wkv_step_kernel(*refs, has_acc, slab, n_slabs):
    r_ref, k_ref, v_ref, kk_ref, ka_ref, ld_ref, g_ref, lnw_ref, lnb_ref, rk_ref, s_ref = refs[:11]
    z_ref, s_out_ref, vt_ref, y_ref = refs[11 + has_acc:]
    _zero_other_slabs(s_out_ref, slab, n_slabs)
    N = RWKV_HEAD
    SUB = 8
    r, k, v, nkk, ka, g = (ref[...].T for ref in (r_ref, k_ref, v_ref, kk_ref, ka_ref, g_ref))
    nkk = -nkk
    w = jnp.exp(ld_ref[...].T)
    vt_ref[...] = v
    sub = lax.broadcasted_iota(jnp.int32, (SUB, 1), 0)
    for hh in range(2):
        rows = slice(hh * N, (hh + 1) * N)
        r_h, k_h, nkk_h, ka_h, w_h = r[rows], k[rows], nkk[rows], ka[rows], w[rows]

        def value_rows(blk, carry, hh=hh, r_h=r_h, k_h=k_h, nkk_h=nkk_h, ka_h=ka_h, w_h=w_h):
            base = pl.multiple_of(hh * N + blk * SUB, SUB)
            v_blk = vt_ref[pl.ds(base, SUB), :]
            y_blk = jnp.zeros((SUB, v_blk.shape[1]), F32)
            for ii in range(SUB):
                s_old = s_ref[0, hh, blk * SUB + ii]
                sa = jnp.sum(s_old * nkk_h, axis=0, keepdims=True)
                s_new = s_old * w_h + sa * ka_h + v_blk[ii:ii + 1, :] * k_h
                s_out_ref[slab, hh, blk * SUB + ii] = s_new
                y_blk = jnp.where(sub == ii, jnp.sum(s_new * r_h, axis=0, keepdims=True), y_blk)
            y_ref[pl.ds(base, SUB), :] = y_blk
            return carry

        lax.fori_loop(0, N // SUB, value_rows, 0)

    def per_head(x, op):
        return jnp.concatenate([jnp.broadcast_to(op(x[hh * N:(hh + 1) * N], axis=0, keepdims=True), (N, x.shape[1]))
                                for hh in range(2)], axis=0)

    y = y_ref[...]
    yc = y - per_head(y, jnp.mean)
    yn = yc * lax.rsqrt(per_head(yc * yc, jnp.mean) + RWKV_GN_EPS)
    bonus = per_head(r * k * rk_ref[...], jnp.sum) * v
    z_ref[...] = ((yn * lnw_ref[...] + lnb_ref[...] + bonus) * g).T


def wkv_step(r, k, v, kk, ka, ld, g, lnw, lnb, rk, states_t, layer, acc):
    n, d = r.shape
    tile = pl.BlockSpec((n, LANES), lambda p: (0, p))
    col = pl.BlockSpec((LANES, 1), lambda p: (p, 0))
    st = pl.BlockSpec((1, 2) + states_t.shape[2:], lambda p: (layer, p, 0, 0, 0))
    ins = [r, k, v, kk, ka, ld, g, lnw, lnb, rk, states_t]
    specs = [tile] * 7 + [col] * 3 + [st]
    aliases = _layer_slab(acc, ins, specs, 1)
    st_out, slab, n_slabs = _slab_spec(acc, layer, states_t.shape[0], (2,) + states_t.shape[2:],
                                       lambda p: (p, 0, 0, 0))
    return pl.pallas_call(
        functools.partial(_wkv_step_kernel, has_acc=acc is not None, slab=slab, n_slabs=n_slabs),
        grid=(d // LANES,),
        in_specs=specs,
        out_specs=[tile, st_out],
        out_shape=[jax.ShapeDtypeStruct((n, d), F32), jax.ShapeDtypeStruct(states_t.shape, F32)],
        scratch_shapes=[pltpu.VMEM((LANES, n), F32), pltpu.VMEM((LANES, n), F32)],
        input_output_aliases=aliases,
        compiler_params=_params("parallel"),
        name="wkv_step",
    )(*ins)


def _trunk(x, seq, states, p):
    depth = p['norm_mix'].shape[0]
    rows = x.shape[0]
    if seq is not None:
        n_seq, tp, lead = seq
        tm = _row_tile(tp, 1088)
        tm_small = _row_tile(tp, 544)
        tiles, tiles_small = tp // tm, tp // tm_small
        cos, sin = rot_table(tp, -lead, 1)
    else:
        ret_s, wkv_s, shift_s, conv_s = states
        wkv_s = jnp.transpose(wkv_s, (0, 2, 3, 4, 1))
        tm = tm_small = rows
        tiles = tiles_small = 1
        lead = 0
        cos, sin = rot_table(8, PAST_LEN, 0)
    new_ret = new_wkv = None
    new_shift, new_conv = [], []
    n_ret, n_rwkv = (depth + 1) // 2, depth // 2
    v_first = None
    for i in range(depth):
        j = i // 2
        gn = p['norm_mix'][i][None]
        if i % 2 == 0:
            gnw = p['ret_gn_w'][j][None]
            if seq is not None:
                proj = norm_proj(x, gn, p['ret_w_in'][j], tm, 1024, BF16)
                o, new_ret = retention_prompt(proj, cos, sin, gnw, n_seq, tp // RET_CHUNK, j, n_ret, new_ret)
            else:
                proj = norm_proj(x, gn, p['ret_w_in'][j], tm, 1024, F32)
                o, new_ret = retention_step(proj[:, None, :], cos, sin, gnw, ret_s, j, new_ret)
                o = o[:, 0, :]
            x = matmul_res(o, p['ret_w_out'][j], x, tm_small, tiles_small, lead)
        else:
            vecs = [p['rwkv_w0'][j], p['rwkv_a0'][j], p['rwkv_k_k'][j], p['rwkv_k_a'][j]]
            vecs.append(p['rwkv_v0'][j - 1] if j else jnp.zeros_like(vecs[0]))
            vecs = jnp.stack(vecs + [jnp.zeros_like(vecs[0])] * 3)
            wts = [p['rwkv_mu'][j], p['rwkv_w_rkv'][j], p['rwkv_w1'][j], p['rwkv_w2'][j], p['rwkv_a1'][j],
                   p['rwkv_a2'][j], p['rwkv_g1'][j], p['rwkv_g2'][j], vecs]
            vres = (v_first, p['rwkv_v1'][j - 1], p['rwkv_v2'][j - 1]) if j else None
            lnw, lnb = p['rwkv_ln_w'][j][None], p['rwkv_ln_b'][j][None]
            rk = p['rwkv_r_k'][j].reshape(1, D_MODEL)
            if seq is not None:
                r, k, v, kk, ka, ld, g, sh = rwkv_proj(x, gn, None, wts, vres, tm_small, n_seq, tiles_small)
                z, new_wkv = wkv_seq(r, k, v, kk, ka, ld, g, lnw, lnb, rk, n_seq, tp // WKV_CHUNK,
                                     j, n_rwkv, new_wkv)
                sh = sh[:, 0, :]
            else:
                r, k, v, kk, ka, ld, g, sh = rwkv_proj(x, gn, shift_s[j], wts, vres, tm, 1, 1)
                z, new_wkv = wkv_step(r, k, v, kk, ka, ld, g, *(t.reshape(D_MODEL, 1) for t in (lnw, lnb, rk)),
                                      wkv_s, j, new_wkv)
            if v_first is None:
                v_first = v
            new_shift.append(sh)
            x = matmul_res(z, p['rwkv_w_o'][j], x, tm_small, tiles_small, lead)
        gf = p['norm_ffn'][i][None]
        cw, cb = p['ffn_conv_w'][i], p['ffn_conv_b'][i][None]
        if seq is not None:
            a, cbuf = ffn_up_seq(x, gf, p['ffn_w_ug'][i], cw, cb, tm, n_seq, tiles)
        else:
            a, n0, n1 = ffn_up_step(x, gf, p['ffn_w_ug'][i], cw, cb, conv_s[i].reshape(rows, 2 * D_FF), D_FF // 2)
            cbuf = jnp.stack([n0, n1], axis=1)
        new_conv.append(cbuf)
        x = matmul_res(a, p['ffn_w_d'][i], x, tm, tiles, lead)
    if seq is None:
        new_wkv = jnp.transpose(new_wkv, (0, 4, 1, 2, 3))
    return x, new_ret, new_wkv, jnp.stack(new_shift), jnp.stack(new_conv)


def kernel(x_prompt, x_sample, state_ret, state_wkv, state_shift, state_conv, meta_tokens, norm_mix, norm_ffn, norm_final, ret_w_in, ret_gn_w, ret_w_out, rwkv_mu, rwkv_w_rkv, rwkv_w0, rwkv_w1, rwkv_w2, rwkv_a0, rwkv_a1, rwkv_a2, rwkv_v0, rwkv_v1, rwkv_v2, rwkv_g1, rwkv_g2, rwkv_k_k, rwkv_k_a, rwkv_r_k, rwkv_ln_w, rwkv_ln_b, rwkv_w_o, ffn_w_ug, ffn_conv_w, ffn_conv_b, ffn_w_d):
    bf = lambda w: [w[i].astype(BF16) for i in range(w.shape[0])]
    p = dict(norm_mix=norm_mix, norm_ffn=norm_ffn, ret_w_in=bf(ret_w_in), ret_gn_w=ret_gn_w,
             ret_w_out=bf(ret_w_out), rwkv_mu=rwkv_mu, rwkv_w_rkv=bf(rwkv_w_rkv), rwkv_w0=rwkv_w0,
             rwkv_w1=bf(rwkv_w1), rwkv_w2=bf(rwkv_w2), rwkv_a0=rwkv_a0, rwkv_a1=bf(rwkv_a1),
             rwkv_a2=bf(rwkv_a2), rwkv_v0=rwkv_v0, rwkv_v1=bf(rwkv_v1), rwkv_v2=bf(rwkv_v2),
             rwkv_g1=bf(rwkv_g1), rwkv_g2=bf(rwkv_g2), rwkv_k_k=rwkv_k_k, rwkv_k_a=rwkv_k_a,
             rwkv_r_k=rwkv_r_k, rwkv_ln_w=rwkv_ln_w, rwkv_ln_b=rwkv_ln_b, rwkv_w_o=bf(rwkv_w_o),
             ffn_w_ug=bf(ffn_w_ug), ffn_conv_w=ffn_conv_w, ffn_conv_b=ffn_conv_b, ffn_w_d=bf(ffn_w_d))
    B, S, D = x_prompt.shape
    lead = (-N_META) % RET_CHUNK
    tp = lead + N_META + S
    assert tp % RET_CHUNK == 0 and D == D_MODEL
    meta = jnp.broadcast_to(meta_tokens[None].astype(F32), (B, N_META, D))
    xp = jnp.concatenate([jnp.zeros((B, lead, D), F32), meta, x_prompt.astype(F32)], axis=1).reshape(B * tp, D)
    xp, p_ret, p_wkv, p_shift, p_conv = _trunk(xp, (B, tp, lead), None, p)
    y_prompt = final_norm(xp.reshape(B, tp, D), norm_final[None], lead + N_META)

    n_req = x_sample.shape[0]
    xs, s_ret, s_wkv, s_shift, s_conv = _trunk(
        x_sample.reshape(n_req, D).astype(F32), None, (state_ret, state_wkv, state_shift, state_conv), p)
    y_sample = final_norm(xs[None], norm_final[None], 0).reshape(n_req, 1, D)
    return (y_prompt.astype(x_prompt.dtype), y_sample.astype(x_sample.dtype),
            p_ret, p_wkv, p_shift, p_conv, s_ret, s_wkv, s_shift, s_conv)
```

```python
import functools
import math

import jax
import jax.numpy as jnp
from jax import lax
from jax.experimental import pallas as pl
from jax.experimental.pallas import tpu as pltpu

F32 = jnp.float32
BF16 = jnp.bfloat16

D_MODEL = 1024
N_META = 16
PAST_LEN = 16384
RET_HEADS = 4
RET_DK = D_MODEL // RET_HEADS
RET_DV = 2 * RET_DK
RET_VDIM = RET_HEADS * RET_DV
RET_CHUNK = 128
RWKV_HEAD = 64
RWKV_HEADS = D_MODEL // RWKV_HEAD
D_FF = 2816
RMS_EPS = 1e-6
RET_GN_EPS = 1e-5
RWKV_GN_EPS = 64e-5

LANES = 128
WKV_CHUNK = 64
WKV_GROUP = 8
VMEM_LIMIT = 56 * 1024 * 1024

NT_DIMS = (((1,), (1,)), ((), ()))
TN_DIMS = (((0,), (0,)), ((), ()))


def _params(*sem):
    return pltpu.CompilerParams(dimension_semantics=sem, vmem_limit_bytes=VMEM_LIMIT)


def _rms(x, g):
    return x * lax.rsqrt(jnp.mean(x * x, axis=-1, keepdims=True) + RMS_EPS) * g


def _sigmoid(x):
    return 1.0 / (1.0 + jnp.exp(-x))


def _row_tile(rows_per_seq, cap):
    best = None
    for t in range(16, min(rows_per_seq, cap) + 1, 16):
        if rows_per_seq % t == 0:
            best = t
    assert best is not None, rows_per_seq
    return best


def _rot_table_kernel(inv_ref, cos_ref, sin_ref, *, pos0, step):
    rows = lax.broadcasted_iota(jnp.int32, cos_ref.shape, 0)
    pos = (rows * step + pos0).astype(F32)
    ang = pos * inv_ref[...]
    cos_ref[...] = jnp.cos(ang)
    sin_ref[...] = jnp.sin(ang)


def rot_table(n_rows, pos0, step):
    half = RET_DK // 2
    inv = (1.0 / (10000.0 ** jnp.linspace(0.0, 1.0, half, dtype=F32))).reshape(1, half)
    return pl.pallas_call(
        functools.partial(_rot_table_kernel, pos0=pos0, step=step),
        out_shape=(jax.ShapeDtypeStruct((n_rows, half), F32),) * 2,
        name="rot_table",
    )(inv)


def _norm_proj_kernel(x_ref, g_ref, w_ref, o_ref, h_ref):
    @pl.when(pl.program_id(1) == 0)
    def _():
        h_ref[...] = _rms(x_ref[...], g_ref[...]).astype(BF16)

    o_ref[...] = jnp.dot(h_ref[...], w_ref[...], preferred_element_type=F32).astype(o_ref.dtype)


def norm_proj(x, g, w, tm, tn, out_dtype):
    m, d = x.shape
    n = w.shape[1]
    return pl.pallas_call(
        _norm_proj_kernel,
        grid=(m // tm, n // tn),
        in_specs=[pl.BlockSpec((tm, d), lambda i, j: (i, 0)),
                  pl.BlockSpec((1, d), lambda i, j: (0, 0)),
                  pl.BlockSpec((d, tn), lambda i, j: (0, j))],
        out_specs=pl.BlockSpec((tm, tn), lambda i, j: (i, j)),
        out_shape=jax.ShapeDtypeStruct((m, n), out_dtype),
        scratch_shapes=[pltpu.VMEM((tm, d), BF16)],
        compiler_params=_params("parallel", "arbitrary"),
        name="norm_proj",
    )(x, g, w)


def _matmul_res_kernel(a_ref, w_ref, x_ref, o_ref, *, tiles_per_seq, lead):
    y = x_ref[...] + jnp.dot(a_ref[...].astype(BF16), w_ref[...], preferred_element_type=F32)
    if lead:
        tm = y.shape[0]
        row = (pl.program_id(0) % tiles_per_seq) * tm + lax.broadcasted_iota(jnp.int32, (tm, 1), 0)
        y = jnp.where(row >= lead, y, 0.0)
    o_ref[...] = y


def matmul_res(a, w, x, tm, tiles_per_seq, lead):
    m, k = a.shape
    n = w.shape[1]
    return pl.pallas_call(
        functools.partial(_matmul_res_kernel, tiles_per_seq=tiles_per_seq, lead=lead),
        grid=(m // tm,),
        in_specs=[pl.BlockSpec((tm, k), lambda i: (i, 0)),
                  pl.BlockSpec((k, n), lambda i: (0, 0), pipeline_mode=pl.Buffered(1)),
                  pl.BlockSpec((tm, n), lambda i: (i, 0))],
        out_specs=pl.BlockSpec((tm, n), lambda i: (i, 0)),
        out_shape=jax.ShapeDtypeStruct((m, n), F32),
        compiler_params=_params("parallel"),
        name="matmul_res",
    )(a, w, x)


def _final_norm_kernel(x_ref, g_ref, o_ref, *, skip):
    o_ref[0] = _rms(x_ref[0, skip:, :], g_ref[...])


def final_norm(x, g, skip):
    n_seq, rows, d = x.shape
    return pl.pallas_call(
        functools.partial(_final_norm_kernel, skip=skip),
        grid=(n_seq,),
        in_specs=[pl.BlockSpec((1, rows, d), lambda b: (b, 0, 0)),
                  pl.BlockSpec((1, d), lambda b: (0, 0))],
        out_specs=pl.BlockSpec((1, rows - skip, d), lambda b: (b, 0, 0)),
        out_shape=jax.ShapeDtypeStruct((n_seq, rows - skip, d), F32),
        compiler_params=_params("parallel"),
        name="final_norm",
    )(x, g)


def _rotate(x, cos, sin):
    half = RET_DK // 2
    x1, x2 = x[:, :half], x[:, half:]
    return jnp.concatenate([x1 * cos - x2 * sin, x1 * sin + x2 * cos], axis=1)


def _gated_head_norm(o, silu_g, gnw):
    oc = o - jnp.mean(o, axis=-1, keepdims=True)
    on = oc * lax.rsqrt(jnp.mean(oc * oc, axis=-1, keepdims=True) + RET_GN_EPS)
    return silu_g * (on * gnw)


def _log_g(h):
    return math.log(1.0 - 2.0 ** (-5.0 - h))


def _layer_slab(acc, ins, specs, out_index):
    if acc is None:
        return {}
    ins.append(acc)
    specs.append(pl.BlockSpec(memory_space=pl.ANY))
    return {len(ins) - 1: out_index}


def _slab_spec(acc, layer, n_layers, rest_block, rest_index):
    if acc is None:
        return pl.BlockSpec((n_layers,) + rest_block, lambda *g: (0,) + rest_index(*g)), layer, n_layers
    return pl.BlockSpec((1,) + rest_block, lambda *g: (layer,) + rest_index(*g)), 0, 1


def _zero_other_slabs(s_out_ref, slab, n_slabs):
    for other in range(n_slabs):
        if other != slab:
            s_out_ref[other] = jnp.zeros(s_out_ref.shape[1:], s_out_ref.dtype)


def _retention_kernel(*refs, has_acc, slab, n_slabs):
    p_ref, cos_ref, sin_ref, gnw_ref = refs[:4]
    o_ref, s_out_ref, s_ref = refs[4 + has_acc:]
    c = pl.program_id(1)
    L = RET_CHUNK
    heads = range(RET_HEADS)
    qk = RET_HEADS * RET_DK

    @pl.when(c == 0)
    def _():
        s_ref[...] = jnp.zeros_like(s_ref)

    dot = lambda a, b: jnp.dot(a, b, preferred_element_type=F32)
    ri = lax.broadcasted_iota(jnp.int32, (L, 1), 0).astype(F32)
    ci = lax.broadcasted_iota(jnp.int32, (1, L), 1).astype(F32)
    rel = ri - ci
    lg = [_log_g(h) for h in heads]
    cos, sin = cos_ref[...], sin_ref[...]
    q = [_rotate(p_ref[:, h * RET_DK:(h + 1) * RET_DK].astype(F32), cos, sin).astype(BF16) for h in heads]
    kf = [_rotate(p_ref[:, qk + h * RET_DK:qk + (h + 1) * RET_DK].astype(F32), cos, sin) * (RET_DK ** -0.5)
          for h in heads]
    v = [p_ref[:, 2 * qk + h * RET_DV:2 * qk + (h + 1) * RET_DV] for h in heads]
    scores = [lax.dot_general(q[h], kf[h].astype(BF16), NT_DIMS, preferred_element_type=F32)
              * jnp.where(rel >= 0, jnp.exp(lg[h] * jnp.maximum(rel, 0.0)), 0.0) for h in heads]
    s_old = [s_ref[h] for h in heads]
    qs = [dot(q[h], s_old[h].astype(BF16)) for h in heads]
    o = [dot(scores[h].astype(BF16), v[h]) + jnp.exp(lg[h] * (ri + 1.0)) * qs[h] for h in heads]
    kd = [(kf[h] * jnp.exp(lg[h] * (L - 1.0 - ri))).astype(BF16) for h in heads]
    s_new = [math.exp(lg[h] * L) * s_old[h] + lax.dot_general(kd[h], v[h], TN_DIMS, preferred_element_type=F32)
             for h in heads]
    for h in heads:
        s_ref[h] = s_new[h]
    g0 = 2 * qk + RET_VDIM
    for h in heads:
        sl = slice(h * RET_DV, (h + 1) * RET_DV)
        g = p_ref[:, g0 + h * RET_DV:g0 + (h + 1) * RET_DV].astype(F32)
        o_ref[:, sl] = _gated_head_norm(o[h], g * _sigmoid(g), gnw_ref[:, sl]).astype(o_ref.dtype)

    @pl.when(c == pl.num_programs(1) - 1)
    def _():
        s_out_ref[slab, 0] = s_ref[...]
        _zero_other_slabs(s_out_ref, slab, n_slabs)


def retention_prompt(proj, cos, sin, gnw, n_seq, chunks, layer, n_layers, acc):
    L = RET_CHUNK
    m, width = proj.shape
    ins = [proj, cos, sin, gnw]
    specs = [pl.BlockSpec((L, width), lambda b, c: (b * chunks + c, 0)),
             pl.BlockSpec((L, RET_DK // 2), lambda b, c: (c, 0)),
             pl.BlockSpec((L, RET_DK // 2), lambda b, c: (c, 0)),
             pl.BlockSpec((1, RET_VDIM), lambda b, c: (0, 0))]
    aliases = _layer_slab(acc, ins, specs, 1)
    st, slab, n_slabs = _slab_spec(acc, layer, n_layers, (1, RET_HEADS, RET_DK, RET_DV), lambda b, c: (b, 0, 0, 0))
    return pl.pallas_call(
        functools.partial(_retention_kernel, has_acc=acc is not None, slab=slab, n_slabs=n_slabs),
        grid=(n_seq, chunks),
        in_specs=specs,
        out_specs=[pl.BlockSpec((L, RET_VDIM), lambda b, c: (b * chunks + c, 0)), st],
        out_shape=[jax.ShapeDtypeStruct((m, RET_VDIM), BF16),
                   jax.ShapeDtypeStruct((n_layers, n_seq, RET_HEADS, RET_DK, RET_DV), F32)],
        scratch_shapes=[pltpu.VMEM((RET_HEADS, RET_DK, RET_DV), F32)],
        input_output_aliases=aliases,
        compiler_params=_params("parallel", "arbitrary"),
        name="retention_prompt",
    )(*ins)


def _to_column(row, n):
    eye = lax.broadcasted_iota(jnp.int32, (n, n), 0) == lax.broadcasted_iota(jnp.int32, (n, n), 1)
    return jnp.sum(jnp.where(eye, jnp.broadcast_to(row, (n, n)), 0.0), axis=-1, keepdims=True)


def _retention_step_kernel(*refs, has_acc, slab, n_slabs):
    p_ref, cos_ref, sin_ref, gnw_ref, s_ref = refs[:5]
    o_ref, s_out_ref = refs[5 + has_acc:]
    _zero_other_slabs(s_out_ref, slab, n_slabs)
    cos, sin = cos_ref[0:1, :], sin_ref[0:1, :]
    qk = RET_HEADS * RET_DK
    for h in range(RET_HEADS):
        decay = math.exp(_log_g(h))
        q = _rotate(p_ref[0, :, h * RET_DK:(h + 1) * RET_DK], cos, sin)
        k = _rotate(p_ref[0, :, qk + h * RET_DK:qk + (h + 1) * RET_DK], cos, sin) * (RET_DK ** -0.5)
        v = p_ref[0, :, 2 * qk + h * RET_DV:2 * qk + (h + 1) * RET_DV]
        g = p_ref[0, :, 2 * qk + RET_VDIM + h * RET_DV:2 * qk + RET_VDIM + (h + 1) * RET_DV]
        s_old = s_ref[0, 0, h]
        qs = jnp.sum(s_old * _to_column(q, RET_DK), axis=0, keepdims=True)
        o = jnp.sum(q * k, axis=-1, keepdims=True) * v + decay * qs
        s_out_ref[slab, 0, h] = decay * s_old + _to_column(k, RET_DK) * v
        o_ref[0, :, h * RET_DV:(h + 1) * RET_DV] = _gated_head_norm(
            o, g * _sigmoid(g), gnw_ref[:, h * RET_DV:(h + 1) * RET_DV]).astype(o_ref.dtype)


def retention_step(proj, cos, sin, gnw, states, layer, acc):
    n = proj.shape[0]
    st = pl.BlockSpec((1, 1, RET_HEADS, RET_DK, RET_DV), lambda b: (layer, b, 0, 0, 0))
    ins = [proj, cos, sin, gnw, states]
    specs = [pl.BlockSpec((1, 1, proj.shape[2]), lambda b: (b, 0, 0)),
             pl.BlockSpec(cos.shape, lambda b: (0, 0)),
             pl.BlockSpec(sin.shape, lambda b: (0, 0)),
             pl.BlockSpec((1, RET_VDIM), lambda b: (0, 0)),
             st]
    aliases = _layer_slab(acc, ins, specs, 1)
    st_out, slab, n_slabs = _slab_spec(acc, layer, states.shape[0], (1, RET_HEADS, RET_DK, RET_DV),
                                       lambda b: (b, 0, 0, 0))
    return pl.pallas_call(
        functools.partial(_retention_step_kernel, has_acc=acc is not None, slab=slab, n_slabs=n_slabs),
        grid=(n,),
        in_specs=specs,
        out_specs=[pl.BlockSpec((1, 1, RET_VDIM), lambda b: (b, 0, 0)), st_out],
        out_shape=[jax.ShapeDtypeStruct((n, 1, RET_VDIM), BF16),
                   jax.ShapeDtypeStruct(states.shape, F32)],
        input_output_aliases=aliases,
        compiler_params=_params("parallel"),
        name="retention_step",
    )(*ins)


FFN_COLS = 256


def _conv_gate(gate, p1, p2, cw, cb):
    rows = lax.broadcasted_iota(jnp.int32, (gate.shape[0], 1), 0)
    g1 = jnp.where(rows == 0, p1, pltpu.roll(gate, 1, axis=0))
    g2 = jnp.where(rows == 0, p2, jnp.where(rows == 1, p1, pltpu.roll(gate, 2, axis=0)))
    return cb + g2 * cw[0:1, :] + g1 * cw[1:2, :] + gate * cw[2:3, :]


def _ffn_up_seq_kernel(x_ref, o_ref, wo_ref, g_ref, w_ref, cw_ref, cb_ref, x1_ref, a_ref, nc_ref, carry_ref,
                       *, tiles_per_seq, lead):
    m = pl.program_id(0)

    @pl.when(m % tiles_per_seq == 0)
    def _():
        carry_ref[...] = jnp.zeros_like(carry_ref)

    x1 = x_ref[...] + jnp.dot(o_ref[...], wo_ref[...], preferred_element_type=F32)
    tm = x1.shape[0]
    row = (m % tiles_per_seq) * tm + lax.broadcasted_iota(jnp.int32, (tm, 1), 0)
    x1 = jnp.where(row >= lead, x1, 0.0)
    x1_ref[...] = x1
    h = _rms(x1, g_ref[...]).astype(BF16)
    for j in range(D_FF // FFN_COLS):
        sl = slice(j * FFN_COLS, (j + 1) * FFN_COLS)
        u = jnp.dot(h, w_ref[:, sl], preferred_element_type=F32)
        gate = jnp.dot(h, w_ref[:, D_FF + j * FFN_COLS:D_FF + (j + 1) * FFN_COLS], preferred_element_type=F32)
        conv = _conv_gate(gate, carry_ref[1:2, sl], carry_ref[0:1, sl], cw_ref[:, sl], cb_ref[:, sl])
        a_ref[:, sl] = (conv * _sigmoid(conv) * u).astype(a_ref.dtype)
        last = gate[tm - 2:tm, :]
        carry_ref[0:2, sl] = last
        nc_ref[0, :, sl] = last


def ffn_up_seq(x, o, w_o, g, w_ug, cw, cb, tm, n_seq, tiles_per_seq, lead):
    m, d = x.shape
    const = lambda a: pl.BlockSpec(a.shape, lambda i: (0, 0), pipeline_mode=pl.Buffered(1))
    return pl.pallas_call(
        functools.partial(_ffn_up_seq_kernel, tiles_per_seq=tiles_per_seq, lead=lead),
        grid=(m // tm,),
        in_specs=[pl.BlockSpec((tm, d), lambda i: (i, 0)),
                  pl.BlockSpec((tm, o.shape[1]), lambda i: (i, 0)),
                  const(w_o), const(g), const(w_ug), const(cw), const(cb)],
        out_specs=[pl.BlockSpec((tm, d), lambda i: (i, 0)),
                   pl.BlockSpec((tm, D_FF), lambda i: (i, 0)),
                   pl.BlockSpec((1, 2, D_FF), lambda i: (i // tiles_per_seq, 0, 0))],
        out_shape=[jax.ShapeDtypeStruct((m, d), F32),
                   jax.ShapeDtypeStruct((m, D_FF), BF16),
                   jax.ShapeDtypeStruct((n_seq, 2, D_FF), F32)],
        scratch_shapes=[pltpu.VMEM((8, D_FF), F32)],
        compiler_params=_params("arbitrary"),
        name="ffn_up_seq",
    )(x, o, w_o, g, w_ug, cw, cb)


def _ffn_up_step_kernel(x_ref, g_ref, wu_ref, wg_ref, cw_ref, cb_ref, b0_ref, b1_ref,
                        a_ref, n0_ref, n1_ref, h_ref):
    @pl.when(pl.program_id(0) == 0)
    def _():
        h_ref[...] = _rms(x_ref[...], g_ref[...]).astype(BF16)

    h = h_ref[...]
    u = jnp.dot(h, wu_ref[...], preferred_element_type=F32)
    gate = jnp.dot(h, wg_ref[...], preferred_element_type=F32)
    b1 = b1_ref[...]
    conv = cb_ref[...] + b0_ref[...] * cw_ref[0:1, :] + b1 * cw_ref[1:2, :] + gate * cw_ref[2:3, :]
    a_ref[...] = (conv * _sigmoid(conv) * u).astype(a_ref.dtype)
    n0_ref[...] = b1
    n1_ref[...] = gate


def ffn_up_step(x, g, w_ug, cw, cb, buf, tn):
    n_req, d = x.shape
    nt = D_FF // tn
    col = pl.BlockSpec((n_req, tn), lambda j: (0, j))
    return pl.pallas_call(
        _ffn_up_step_kernel,
        grid=(nt,),
        in_specs=[pl.BlockSpec((n_req, d), lambda j: (0, 0)),
                  pl.BlockSpec((1, d), lambda j: (0, 0)),
                  pl.BlockSpec((d, tn), lambda j: (0, j)),
                  pl.BlockSpec((d, tn), lambda j: (0, nt + j)),
                  pl.BlockSpec((3, tn), lambda j: (0, j)),
                  pl.BlockSpec((1, tn), lambda j: (0, j)),
                  col,
                  pl.BlockSpec((n_req, tn), lambda j: (0, nt + j))],
        out_specs=[col, col, col],
        out_shape=[jax.ShapeDtypeStruct((n_req, D_FF), BF16),
                   jax.ShapeDtypeStruct((n_req, D_FF), F32),
                   jax.ShapeDtypeStruct((n_req, D_FF), F32)],
        scratch_shapes=[pltpu.VMEM((n_req, d), BF16)],
        compiler_params=_params("arbitrary"),
        name="ffn_up_step",
    )(x, g, w_ug, w_ug, cw, cb, buf, buf)


def _pair_ones():
    r = lax.broadcasted_iota(jnp.int32, (LANES, LANES), 0) // RWKV_HEAD
    c = lax.broadcasted_iota(jnp.int32, (LANES, LANES), 1) // RWKV_HEAD
    return jnp.where(r == c, 1.0, 0.0).astype(BF16)


def _head_sum(x, ones, terms):
    total = None
    for _ in range(terms):
        piece = x.astype(BF16)
        part = jnp.dot(piece, ones, preferred_element_type=F32)
        total = part if total is None else total + part
        x = x - piece.astype(F32)
    return total


def _rwkv_proj_body(h, hprev, mu_ref, wrkv_ref, w1_ref, w2_ref, a1_ref, a2_ref, g1_ref, g2_ref, vec_ref,
                    vres, outs, mix_dtype):
    r_ref, k_ref, v_ref, kk_ref, ka_ref, ld_ref, g_ref = outs
    hm, dxm = h.astype(mix_dtype), (hprev - h).astype(mix_dtype)
    mix = lambda i: (hm + dxm * mu_ref[i:i + 1, :].astype(mix_dtype)).astype(BF16)
    xr, xw, xk, xv, xa, xg = (mix(i) for i in range(6))
    dot = lambda a, b: jnp.dot(a, b, preferred_element_type=F32)
    w0, a0, k_k, k_a = (vec_ref[i:i + 1, :] for i in range(4))
    r = dot(xr, wrkv_ref[0])
    k = dot(xk, wrkv_ref[1])
    v = dot(xv, wrkv_ref[2])
    z = w0 + dot(jnp.tanh(dot(xw, w1_ref[...])).astype(BF16), w2_ref[...])
    ld_ref[...] = -math.exp(-0.5) * _sigmoid(z)
    a = _sigmoid(a0 + dot(dot(xa, a1_ref[...]).astype(BF16), a2_ref[...]))
    g_ref[...] = dot(_sigmoid(dot(xg, g1_ref[...])).astype(BF16), g2_ref[...]).astype(g_ref.dtype)
    if vres is not None:
        vf_ref, v1_ref, v2_ref = vres
        v0 = vec_ref[4:5, :]
        v = v + (vf_ref[...] - v) * _sigmoid(v0 + dot(dot(xv, v1_ref[...]).astype(BF16), v2_ref[...]))
    kkr = k * k_k
    ones = _pair_ones()
    kk = jnp.concatenate(
        [kkr[:, t:t + LANES] * lax.rsqrt(jnp.maximum(
            _head_sum(kkr[:, t:t + LANES] * kkr[:, t:t + LANES], ones, 2), 1e-12))
         for t in range(0, D_MODEL, LANES)], axis=1)
    r_ref[...] = r.astype(r_ref.dtype)
    k_ref[...] = (k * (1.0 + (a - 1.0) * k_a)).astype(k_ref.dtype)
    v_ref[...] = v.astype(v_ref.dtype)
    kk_ref[...] = kk.astype(kk_ref.dtype)
    ka_ref[...] = (kk * a).astype(ka_ref.dtype)


def _rwkv_proj_seq_kernel(*refs, tiles_per_seq, has_vres):
    x_ref, gn_ref = refs[0], refs[1]
    weights = refs[2:11]
    n_in = 11 + (3 if has_vres else 0)
    vres = refs[11:14] if has_vres else None
    outs = refs[n_in:n_in + 7]
    shift_ref, carry_ref = refs[n_in + 7], refs[n_in + 8]
    m = pl.program_id(0)

    @pl.when(m % tiles_per_seq == 0)
    def _():
        carry_ref[...] = jnp.zeros_like(carry_ref)

    h = _rms(x_ref[...], gn_ref[...])
    tm = h.shape[0]
    rows = lax.broadcasted_iota(jnp.int32, (tm, 1), 0)
    hprev = jnp.where(rows == 0, carry_ref[0:1, :], pltpu.roll(h, 1, axis=0))
    _rwkv_proj_body(h, hprev, *weights, vres, outs, BF16)
    carry_ref[0:1, :] = h[tm - 1:tm, :]
    shift_ref[0] = h[tm - 1:tm, :]


def _rwkv_proj_step_kernel(*refs, has_vres):
    x_ref, gn_ref, prev_ref = refs[0], refs[1], refs[2]
    weights = refs[3:12]
    n_in = 12 + (3 if has_vres else 0)
    vres = refs[12:15] if has_vres else None
    outs = refs[n_in:n_in + 7]
    shift_ref = refs[n_in + 7]
    h = _rms(x_ref[...], gn_ref[...])
    _rwkv_proj_body(h, prev_ref[...], *weights, vres, outs, F32)
    shift_ref[...] = h


def rwkv_proj(x, gn, prev, wts, vres, tm, n_seq, tiles_per_seq):
    m, d = x.shape
    full = lambda a: pl.BlockSpec(a.shape, lambda i, _n=a.ndim: (0,) * _n, pipeline_mode=pl.Buffered(1))
    rowblk = pl.BlockSpec((tm, d), lambda i: (i, 0))
    seq = prev is None
    ins = [x, gn] + ([] if seq else [prev]) + list(wts)
    specs = [rowblk, full(gn)] + ([] if seq else [rowblk]) + [full(a) for a in wts]
    if vres is not None:
        vf, v1, v2 = vres
        ins += [vf, v1, v2]
        specs += [rowblk, full(v1), full(v2)]
    act = BF16 if seq else F32
    out_shape = [jax.ShapeDtypeStruct((m, d), dt) for dt in (act, F32, F32, F32, F32, F32, act)]
    out_specs = [rowblk] * 7
    if seq:
        out_shape.append(jax.ShapeDtypeStruct((n_seq, 1, d), F32))
        out_specs.append(pl.BlockSpec((1, 1, d), lambda i: (i // tiles_per_seq, 0, 0)))
        body = functools.partial(_rwkv_proj_seq_kernel, tiles_per_seq=tiles_per_seq, has_vres=vres is not None)
        scratch = [pltpu.VMEM((8, d), F32)]
    else:
        out_shape.append(jax.ShapeDtypeStruct((m, d), F32))
        out_specs.append(rowblk)
        body = functools.partial(_rwkv_proj_step_kernel, has_vres=vres is not None)
        scratch = []
    return pl.pallas_call(
        body,
        grid=(m // tm,),
        in_specs=specs,
        out_specs=out_specs,
        out_shape=out_shape,
        scratch_shapes=scratch,
        compiler_params=_params("arbitrary"),
        name="rwkv_proj_seq" if seq else "rwkv_proj_step",
    )(*ins)


def _wkv_epilogue(y, r, k, v, g, lnw, lnb, rk):
    inv_n = 1.0 / RWKV_HEAD
    first = lax.broadcasted_iota(jnp.int32, (1, LANES), 1) < RWKV_HEAD

    def head_sum(x):
        s0 = jnp.sum(jnp.where(first, x, 0.0), axis=-1, keepdims=True)
        s1 = jnp.sum(jnp.where(first, 0.0, x), axis=-1, keepdims=True)
        return jnp.where(first, s0, s1)

    yc = y - head_sum(y) * inv_n
    yn = yc * lax.rsqrt(head_sum(yc * yc) * inv_n + RWKV_GN_EPS)
    return (yn * lnw + lnb + head_sum(r * k * rk) * v) * g


def _stack_heads(x):
    first = lax.broadcasted_iota(jnp.int32, (1, LANES), 1) < RWKV_HEAD
    return jnp.concatenate([jnp.where(first, x, 0.0), jnp.where(first, 0.0, x)], axis=0)


def _wkv_seq_kernel(*refs, pairs, has_acc, slab, n_slabs):
    r_ref, k_ref, v_ref, kk_ref, ka_ref, ld_ref, g_ref, lnw_ref, lnb_ref, rk_ref = refs[:10]
    z_ref, s_out_ref, s_ref = refs[10 + has_acc:]
    c = pl.program_id(1)
    L = WKV_CHUNK
    R = 2 * L

    @pl.when(c == 0)
    def _():
        s_ref[...] = jnp.zeros_like(s_ref)

    dot = lambda a, b: jnp.dot(a.astype(BF16), b.astype(BF16), preferred_element_type=F32)
    dot_nt = lambda a, b: lax.dot_general(a.astype(BF16), b.astype(BF16), NT_DIMS, preferred_element_type=F32)
    dot_tn = lambda a, b: lax.dot_general(a.astype(BF16), b.astype(BF16), TN_DIMS, preferred_element_type=F32)
    ti = lax.broadcasted_iota(jnp.int32, (L, L), 0)
    tj = lax.broadcasted_iota(jnp.int32, (L, L), 1)
    tri = jnp.where(ti >= tj, 1.0, 0.0).astype(BF16)
    ri = lax.broadcasted_iota(jnp.int32, (R, R), 0)
    rj = lax.broadcasted_iota(jnp.int32, (R, R), 1)
    same = (ri // L) == (rj // L)
    lower = same & ((ri % L) > (rj % L))
    lower_eq = same & ((ri % L) >= (rj % L))
    eye = jnp.where(ri == rj, 1.0, 0.0)

    for first in range(0, pairs, WKV_GROUP):
        _wkv_group(range(first, first + WKV_GROUP), refs[:10], z_ref, s_out_ref, s_ref, c,
                   (dot, dot_nt, dot_tn, tri, lower, lower_eq, eye))

    @pl.when(c == pl.num_programs(1) - 1)
    def _():
        for p in range(pairs):
            s_out_ref[slab, 0, 2 * p] = s_ref[p, 0:RWKV_HEAD, 0:RWKV_HEAD]
            s_out_ref[slab, 0, 2 * p + 1] = s_ref[p, RWKV_HEAD:LANES, RWKV_HEAD:LANES]
        _zero_other_slabs(s_out_ref, slab, n_slabs)


def _wkv_group(P, ins, z_ref, s_out_ref, s_ref, c, consts):
    r_ref, k_ref, v_ref, kk_ref, ka_ref, ld_ref, g_ref, lnw_ref, lnb_ref, rk_ref = ins
    dot, dot_nt, dot_tn, tri, lower, lower_eq, eye = consts
    L = WKV_CHUNK
    R = 2 * L
    P = list(P)
    sls = {p: slice(p * LANES, (p + 1) * LANES) for p in P}
    ld = {p: ld_ref[:, sls[p]] for p in P}
    ld_hi = {p: ld[p].astype(BF16) for p in P}
    ld_mid = {p: (ld[p] - ld_hi[p].astype(F32)).astype(BF16) for p in P}
    ld_lo = {p: (ld[p] - ld_hi[p].astype(F32) - ld_mid[p].astype(F32)).astype(BF16) for p in P}
    cum = {p: dot(tri, ld_hi[p]) + dot(tri, ld_mid[p]) + dot(tri, ld_lo[p]) for p in P}
    tot = {p: cum[p][L - 1:L, :] for p in P}
    ar, bk, bk_end, v_s = {}, {}, {}, {}
    for p in P:
        r, k, v, kk, ka = (ref[:, sls[p]] for ref in (r_ref, k_ref, v_ref, kk_ref, ka_ref))
        dec_out = jnp.exp(-cum[p])
        dec_end = jnp.exp(tot[p] - cum[p])
        ar[p] = jnp.concatenate([_stack_heads(jnp.exp(cum[p] - ld[p]) * (-kk)),
                                 _stack_heads(jnp.exp(cum[p]) * r)], axis=0).astype(BF16)
        bk[p] = jnp.concatenate([_stack_heads(dec_out * ka), _stack_heads(dec_out * k)], axis=0).astype(BF16)
        bk_end[p] = jnp.concatenate([_stack_heads(dec_end * ka), _stack_heads(dec_end * k)], axis=0).astype(BF16)
        v_s[p] = _stack_heads(v)
    big = {p: dot_nt(ar[p], bk[p]) for p in P}
    n_ab = {p: jnp.where(lower, big[p][0:R, 0:R], 0.0) for p in P}
    a_ak = {p: jnp.where(lower, big[p][0:R, R:2 * R], 0.0).astype(BF16) for p in P}
    a_r = {p: jnp.concatenate([jnp.where(lower_eq, big[p][R:2 * R, 0:R], 0.0),
                               jnp.where(lower_eq, big[p][R:2 * R, R:2 * R], 0.0)], axis=1).astype(BF16)
           for p in P}
    inv = {p: eye + n_ab[p] for p in P}
    pw = {p: dot(n_ab[p], n_ab[p]) for p in P}
    for _ in range(int(math.log2(L)) - 2):
        both = {p: dot(jnp.concatenate([pw[p], inv[p]], axis=0), pw[p]) for p in P}
        inv = {p: inv[p] + both[p][R:2 * R] for p in P}
        pw = {p: both[p][0:R] for p in P}
    inv = {p: inv[p] + dot(inv[p], pw[p]) for p in P}
    ars = {p: dot_nt(ar[p], s_ref[p]) for p in P}
    akv = {p: dot(a_ak[p], v_s[p]) for p in P}
    u_s = {p: dot(inv[p], ars[p][0:R] + akv[p]) for p in P}
    uv = {p: jnp.concatenate([u_s[p], v_s[p]], axis=0).astype(BF16) for p in P}
    y_s = {p: ars[p][R:2 * R] + dot(a_r[p], uv[p]) for p in P}
    for p in P:
        s_ref[p] = s_ref[p] * jnp.exp(tot[p]) + dot_tn(uv[p], bk_end[p])
    for p in P:
        sl = sls[p]
        y = y_s[p][0:L, :] + y_s[p][L:R, :]
        z_ref[:, sl] = _wkv_epilogue(y, r_ref[:, sl], k_ref[:, sl], v_ref[:, sl], g_ref[:, sl], lnw_ref[:, sl],
                                     lnb_ref[:, sl], rk_ref[:, sl]).astype(z_ref.dtype)


def wkv_seq(r, k, v, kk, ka, ld, g, lnw, lnb, rk, n_seq, chunks, layer, n_layers, acc):
    m, d = r.shape
    L = WKV_CHUNK
    blk = pl.BlockSpec((L, d), lambda b, c: (b * chunks + c, 0))
    vec = pl.BlockSpec((1, d), lambda b, c: (0, 0))
    ins = [r, k, v, kk, ka, ld, g, lnw, lnb, rk]
    specs = [blk] * 7 + [vec] * 3
    aliases = _layer_slab(acc, ins, specs, 1)
    st, slab, n_slabs = _slab_spec(acc, layer, n_layers, (1, RWKV_HEADS, RWKV_HEAD, RWKV_HEAD),
                                   lambda b, c: (b, 0, 0, 0))
    return pl.pallas_call(
        functools.partial(_wkv_seq_kernel, pairs=d // LANES, has_acc=acc is not None, slab=slab, n_slabs=n_slabs),
        grid=(n_seq, chunks),
        in_specs=specs,
        out_specs=[blk, st],
        out_shape=[jax.ShapeDtypeStruct((m, d), BF16),
                   jax.ShapeDtypeStruct((n_layers, n_seq, RWKV_HEADS, RWKV_HEAD, RWKV_HEAD), F32)],
        scratch_shapes=[pltpu.VMEM((d // LANES, LANES, LANES), F32)],
        input_output_aliases=aliases,
        compiler_params=_params("parallel", "arbitrary"),
        name="wkv_seq",
    )(*ins)


def _wkv_step_kernel(*refs, has_acc, slab, n_slabs):
    r_ref, k_ref, v_ref, kk_ref, ka_ref, ld_ref, g_ref, lnw_ref, lnb_ref, rk_ref, s_ref = refs[:11]
    z_ref, s_out_ref, vt_ref, y_ref = refs[11 + has_acc:]
    _zero_other_slabs(s_out_ref, slab, n_slabs)
    N = RWKV_HEAD
    SUB = 8
    r, k, v, nkk, ka, g = (ref[...].T for ref in (r_ref, k_ref, v_ref, kk_ref, ka_ref, g_ref))
    nkk = -nkk
    w = jnp.exp(ld_ref[...].T)
    vt_ref[...] = v
    sub = lax.broadcasted_iota(jnp.int32, (SUB, 1), 0)
    for hh in range(2):
        rows = slice(hh * N, (hh + 1) * N)
        r_h, k_h, nkk_h, ka_h, w_h = r[rows], k[rows], nkk[rows], ka[rows], w[rows]

        def value_rows(blk, carry, hh=hh, r_h=r_h, k_h=k_h, nkk_h=nkk_h, ka_h=ka_h, w_h=w_h):
            base = pl.multiple_of(hh * N + blk * SUB, SUB)
            v_blk = vt_ref[pl.ds(base, SUB), :]
            y_blk = jnp.zeros((SUB, v_blk.shape[1]), F32)
            for ii in range(SUB):
                s_old = s_ref[0, hh, blk * SUB + ii]
                sa = jnp.sum(s_old * nkk_h, axis=0, keepdims=True)
                s_new = s_old * w_h + sa * ka_h + v_blk[ii:ii + 1, :] * k_h
                s_out_ref[slab, hh, blk * SUB + ii] = s_new
                y_blk = jnp.where(sub == ii, jnp.sum(s_new * r_h, axis=0, keepdims=True), y_blk)
            y_ref[pl.ds(base, SUB), :] = y_blk
            return carry

        lax.fori_loop(0, N // SUB, value_rows, 0)

    def per_head(x, op):
        return jnp.concatenate([jnp.broadcast_to(op(x[hh * N:(hh + 1) * N], axis=0, keepdims=True), (N, x.shape[1]))
                                for hh in range(2)], axis=0)

    y = y_ref[...]
    yc = y - per_head(y, jnp.mean)
    yn = yc * lax.rsqrt(per_head(yc * yc, jnp.mean) + RWKV_GN_EPS)
    bonus = per_head(r * k * rk_ref[...], jnp.sum) * v
    z_ref[...] = ((yn * lnw_ref[...] + lnb_ref[...] + bonus) * g).T


def wkv_step(r, k, v, kk, ka, ld, g, lnw, lnb, rk, states_t, layer, acc):
    n, d = r.shape
    tile = pl.BlockSpec((n, LANES), lambda p: (0, p))
    col = pl.BlockSpec((LANES, 1), lambda p: (p, 0))
    st = pl.BlockSpec((1, 2) + states_t.shape[2:], lambda p: (layer, p, 0, 0, 0))
    ins = [r, k, v, kk, ka, ld, g, lnw, lnb, rk, states_t]
    specs = [tile] * 7 + [col] * 3 + [st]
    aliases = _layer_slab(acc, ins, specs, 1)
    st_out, slab, n_slabs = _slab_spec(acc, layer, states_t.shape[0], (2,) + states_t.shape[2:],
                                       lambda p: (p, 0, 0, 0))
    return pl.pallas_call(
        functools.partial(_wkv_step_kernel, has_acc=acc is not None, slab=slab, n_slabs=n_slabs),
        grid=(d // LANES,),
        in_specs=specs,
        out_specs=[tile, st_out],
        out_shape=[jax.ShapeDtypeStruct((n, d), F32), jax.ShapeDtypeStruct(states_t.shape, F32)],
        scratch_shapes=[pltpu.VMEM((LANES, n), F32), pltpu.VMEM((LANES, n), F32)],
        input_output_aliases=aliases,
        compiler_params=_params("parallel"),
        name="wkv_step",
    )(*ins)


def _trunk(x, seq, states, p):
    depth = p['norm_mix'].shape[0]
    rows = x.shape[0]
    if seq is not None:
        n_seq, tp, lead = seq
        tm = _row_tile(tp, 1088)
        tm_small = _row_tile(tp, 544)
        tiles, tiles_small = tp // tm, tp // tm_small
        cos, sin = rot_table(tp, -lead, 1)
    else:
        ret_s, wkv_s, shift_s, conv_s = states
        wkv_s = jnp.transpose(wkv_s, (0, 2, 3, 4, 1))
        tm = tm_small = rows
        tiles = tiles_small = 1
        lead = 0
        cos, sin = rot_table(8, PAST_LEN, 0)
    new_ret = new_wkv = None
    new_shift, new_conv = [], []
    n_ret, n_rwkv = (depth + 1) // 2, depth // 2
    v_first = None
    for i in range(depth):
        j = i // 2
        gn = p['norm_mix'][i][None]
        if i % 2 == 0:
            gnw = p['ret_gn_w'][j][None]
            if seq is not None:
                proj = norm_proj(x, gn, p['ret_w_in'][j], tm, 3072, BF16)
                o, new_ret = retention_prompt(proj, cos, sin, gnw, n_seq, tp // RET_CHUNK, j, n_ret, new_ret)
            else:
                proj = norm_proj(x, gn, p['ret_w_in'][j], tm, 1024, F32)
                o, new_ret = retention_step(proj[:, None, :], cos, sin, gnw, ret_s, j, new_ret)
                o = o[:, 0, :]
            mix_out, w_mix = o, p['ret_w_out'][j]
        else:
            vecs = [p['rwkv_w0'][j], p['rwkv_a0'][j], p['rwkv_k_k'][j], p['rwkv_k_a'][j]]
            vecs.append(p['rwkv_v0'][j - 1] if j else jnp.zeros_like(vecs[0]))
            vecs = jnp.stack(vecs + [jnp.zeros_like(vecs[0])] * 3)
            wts = [p['rwkv_mu'][j], p['rwkv_w_rkv'][j], p['rwkv_w1'][j], p['rwkv_w2'][j], p['rwkv_a1'][j],
                   p['rwkv_a2'][j], p['rwkv_g1'][j], p['rwkv_g2'][j], vecs]
            vres = (v_first, p['rwkv_v1'][j - 1], p['rwkv_v2'][j - 1]) if j else None
            lnw, lnb = p['rwkv_ln_w'][j][None], p['rwkv_ln_b'][j][None]
            rk = p['rwkv_r_k'][j].reshape(1, D_MODEL)
            if seq is not None:
                r, k, v, kk, ka, ld, g, sh = rwkv_proj(x, gn, None, wts, vres, tm_small, n_seq, tiles_small)
                z, new_wkv = wkv_seq(r, k, v, kk, ka, ld, g, lnw, lnb, rk, n_seq, tp // WKV_CHUNK,
                                     j, n_rwkv, new_wkv)
                sh = sh[:, 0, :]
            else:
                r, k, v, kk, ka, ld, g, sh = rwkv_proj(x, gn, shift_s[j], wts, vres, tm, 1, 1)
                z, new_wkv = wkv_step(r, k, v, kk, ka, ld, g, *(t.reshape(D_MODEL, 1) for t in (lnw, lnb, rk)),
                                      wkv_s, j, new_wkv)
            if v_first is None:
                v_first = v
            new_shift.append(sh)
            mix_out, w_mix = z, p['rwkv_w_o'][j]
        gf = p['norm_ffn'][i][None]
        cw, cb = p['ffn_conv_w'][i], p['ffn_conv_b'][i][None]
        if seq is not None:
            x, a, cbuf = ffn_up_seq(x, mix_out, w_mix, gf, p['ffn_w_ug'][i], cw, cb, tm_small, n_seq,
                                    tiles_small, lead)
        else:
            x = matmul_res(mix_out, w_mix, x, tm_small, tiles_small, lead)
            a, n0, n1 = ffn_up_step(x, gf, p['ffn_w_ug'][i], cw, cb, conv_s[i].reshape(rows, 2 * D_FF), D_FF // 2)
            cbuf = jnp.stack([n0, n1], axis=1)
        new_conv.append(cbuf)
        x = matmul_res(a, p['ffn_w_d'][i], x, tm, tiles, lead)
    if seq is None:
        new_wkv = jnp.transpose(new_wkv, (0, 4, 1, 2, 3))
    return x, new_ret, new_wkv, jnp.stack(new_shift), jnp.stack(new_conv)


def kernel(x_prompt, x_sample, state_ret, state_wkv, state_shift, state_conv, meta_tokens, norm_mix, norm_ffn, norm_final, ret_w_in, ret_gn_w, ret_w_out, rwkv_mu, rwkv_w_rkv, rwkv_w0, rwkv_w1, rwkv_w2, rwkv_a0, rwkv_a1, rwkv_a2, rwkv_v0, rwkv_v1, rwkv_v2, rwkv_g1, rwkv_g2, rwkv_k_k, rwkv_k_a, rwkv_r_k, rwkv_ln_w, rwkv_ln_b, rwkv_w_o, ffn_w_ug, ffn_conv_w, ffn_conv_b, ffn_w_d):
    bf = lambda w: [w[i].astype(BF16) for i in range(w.shape[0])]
    p = dict(norm_mix=norm_mix, norm_ffn=norm_ffn, ret_w_in=bf(ret_w_in), ret_gn_w=ret_gn_w,
             ret_w_out=bf(ret_w_out), rwkv_mu=rwkv_mu, rwkv_w_rkv=bf(rwkv_w_rkv), rwkv_w0=rwkv_w0,
             rwkv_w1=bf(rwkv_w1), rwkv_w2=bf(rwkv_w2), rwkv_a0=rwkv_a0, rwkv_a1=bf(rwkv_a1),
             rwkv_a2=bf(rwkv_a2), rwkv_v0=rwkv_v0, rwkv_v1=bf(rwkv_v1), rwkv_v2=bf(rwkv_v2),
             rwkv_g1=bf(rwkv_g1), rwkv_g2=bf(rwkv_g2), rwkv_k_k=rwkv_k_k, rwkv_k_a=rwkv_k_a,
             rwkv_r_k=rwkv_r_k, rwkv_ln_w=rwkv_ln_w, rwkv_ln_b=rwkv_ln_b, rwkv_w_o=bf(rwkv_w_o),
             ffn_w_ug=bf(ffn_w_ug), ffn_conv_w=ffn_conv_w, ffn_conv_b=ffn_conv_b, ffn_w_d=bf(ffn_w_d))
    B, S, D = x_prompt.shape
    lead = (-N_META) % RET_CHUNK
    tp = lead + N_META + S
    assert tp % RET_CHUNK == 0 and D == D_MODEL
    meta = jnp.broadcast_to(meta_tokens[None].astype(F32), (B, N_META, D))
    xp = jnp.concatenate([jnp.zeros((B, lead, D), F32), meta, x_prompt.astype(F32)], axis=1).reshape(B * tp, D)
    xp, p_ret, p_wkv, p_shift, p_conv = _trunk(xp, (B, tp, lead), None, p)
    y_prompt = final_norm(xp.reshape(B, tp, D), norm_final[None], lead + N_META)

    n_req = x_sample.shape[0]
    xs, s_ret, s_wkv, s_shift, s_conv = _trunk(
        x_sample.reshape(n_req, D).astype(F32), None, (state_ret, state_wkv, state_shift, state_conv), p)
    y_sample = final_norm(xs[None], norm_final[None], 0).reshape(n_req, 1, D)
    return (y_prompt.astype(x_prompt.dtype), y_sample.astype(x_sample.dtype),
            p_ret, p_wkv, p_shift, p_conv, s_ret, s_wkv, s_shift, s_conv)
```

```python
import functools
import math

import jax
import jax.numpy as jnp
from jax import lax
from jax.experimental import pallas as pl
from jax.experimental.pallas import tpu as pltpu

F32 = jnp.float32
BF16 = jnp.bfloat16

D_MODEL = 1024
N_META = 16
PAST_LEN = 16384
RET_HEADS = 4
RET_DK = D_MODEL // RET_HEADS
RET_DV = 2 * RET_DK
RET_VDIM = RET_HEADS * RET_DV
RET_CHUNK = 128
RWKV_HEAD = 64
RWKV_HEADS = D_MODEL // RWKV_HEAD
D_FF = 2816
RMS_EPS = 1e-6
RET_GN_EPS = 1e-5
RWKV_GN_EPS = 64e-5

LANES = 128
WKV_CHUNK = 64
WKV_GROUP = 8
VMEM_LIMIT = 56 * 1024 * 1024

NT_DIMS = (((1,), (1,)), ((), ()))
TN_DIMS = (((0,), (0,)), ((), ()))


def _params(*sem):
    return pltpu.CompilerParams(dimension_semantics=sem, vmem_limit_bytes=VMEM_LIMIT)


def _rms(x, g):
    return x * lax.rsqrt(jnp.mean(x * x, axis=-1, keepdims=True) + RMS_EPS) * g


def _sigmoid(x):
    return 1.0 / (1.0 + jnp.exp(-x))


def _row_tile(rows_per_seq, cap):
    best = None
    for t in range(16, min(rows_per_seq, cap) + 1, 16):
        if rows_per_seq % t == 0:
            best = t
    assert best is not None, rows_per_seq
    return best


def _rot_table_kernel(inv_ref, cos_ref, sin_ref, *, pos0, step):
    rows = lax.broadcasted_iota(jnp.int32, cos_ref.shape, 0)
    pos = (rows * step + pos0).astype(F32)
    ang = pos * inv_ref[...]
    cos_ref[...] = jnp.cos(ang)
    sin_ref[...] = jnp.sin(ang)


def rot_table(n_rows, pos0, step):
    half = RET_DK // 2
    inv = (1.0 / (10000.0 ** jnp.linspace(0.0, 1.0, half, dtype=F32))).reshape(1, half)
    return pl.pallas_call(
        functools.partial(_rot_table_kernel, pos0=pos0, step=step),
        out_shape=(jax.ShapeDtypeStruct((n_rows, half), F32),) * 2,
        name="rot_table",
    )(inv)


def _norm_proj_kernel(x_ref, g_ref, w_ref, o_ref, h_ref):
    @pl.when(pl.program_id(1) == 0)
    def _():
        h_ref[...] = _rms(x_ref[...], g_ref[...]).astype(BF16)

    o_ref[...] = jnp.dot(h_ref[...], w_ref[...], preferred_element_type=F32).astype(o_ref.dtype)


def norm_proj(x, g, w, layer, tm, tn, out_dtype):
    m, d = x.shape
    n = w.shape[2]
    return pl.pallas_call(
        _norm_proj_kernel,
        grid=(m // tm, n // tn),
        in_specs=[pl.BlockSpec((tm, d), lambda i, j: (i, 0)),
                  pl.BlockSpec((1, d), lambda i, j: (0, 0)),
                  pl.BlockSpec((None, d, tn), lambda i, j: (layer, 0, j))],
        out_specs=pl.BlockSpec((tm, tn), lambda i, j: (i, j)),
        out_shape=jax.ShapeDtypeStruct((m, n), out_dtype),
        scratch_shapes=[pltpu.VMEM((tm, d), BF16)],
        compiler_params=_params("parallel", "arbitrary"),
        name="norm_proj",
    )(x, g, w)


def _matmul_res_kernel(a_ref, w_ref, x_ref, o_ref, *, tiles_per_seq, lead):
    y = x_ref[...] + jnp.dot(a_ref[...].astype(BF16), w_ref[...], preferred_element_type=F32)
    if lead:
        tm = y.shape[0]
        row = (pl.program_id(0) % tiles_per_seq) * tm + lax.broadcasted_iota(jnp.int32, (tm, 1), 0)
        y = jnp.where(row >= lead, y, 0.0)
    o_ref[...] = y


def matmul_res(a, w, layer, x, tm, tiles_per_seq, lead):
    m, k = a.shape
    n = w.shape[2]
    return pl.pallas_call(
        functools.partial(_matmul_res_kernel, tiles_per_seq=tiles_per_seq, lead=lead),
        grid=(m // tm,),
        in_specs=[pl.BlockSpec((tm, k), lambda i: (i, 0)),
                  pl.BlockSpec((None, k, n), lambda i: (layer, 0, 0), pipeline_mode=pl.Buffered(1)),
                  pl.BlockSpec((tm, n), lambda i: (i, 0))],
        out_specs=pl.BlockSpec((tm, n), lambda i: (i, 0)),
        out_shape=jax.ShapeDtypeStruct((m, n), F32),
        compiler_params=_params("parallel"),
        name="matmul_res",
    )(a, w, x)


def _final_norm_kernel(x_ref, g_ref, o_ref, *, skip):
    o_ref[0] = _rms(x_ref[0, skip:, :], g_ref[...])


def final_norm(x, g, skip):
    n_seq, rows, d = x.shape
    return pl.pallas_call(
        functools.partial(_final_norm_kernel, skip=skip),
        grid=(n_seq,),
        in_specs=[pl.BlockSpec((1, rows, d), lambda b: (b, 0, 0)),
                  pl.BlockSpec((1, d), lambda b: (0, 0))],
        out_specs=pl.BlockSpec((1, rows - skip, d), lambda b: (b, 0, 0)),
        out_shape=jax.ShapeDtypeStruct((n_seq, rows - skip, d), F32),
        compiler_params=_params("parallel"),
        name="final_norm",
    )(x, g)


def _rotate(x, cos, sin):
    half = RET_DK // 2
    x1, x2 = x[:, :half], x[:, half:]
    return jnp.concatenate([x1 * cos - x2 * sin, x1 * sin + x2 * cos], axis=1)


def _gated_head_norm(o, silu_g, gnw):
    oc = o - jnp.mean(o, axis=-1, keepdims=True)
    on = oc * lax.rsqrt(jnp.mean(oc * oc, axis=-1, keepdims=True) + RET_GN_EPS)
    return silu_g * (on * gnw)


def _log_g(h):
    return math.log(1.0 - 2.0 ** (-5.0 - h))


def _layer_slab(acc, ins, specs, out_index):
    if acc is None:
        return {}
    ins.append(acc)
    specs.append(pl.BlockSpec(memory_space=pl.ANY))
    return {len(ins) - 1: out_index}


def _slab_spec(acc, layer, n_layers, rest_block, rest_index):
    if acc is None:
        return pl.BlockSpec((n_layers,) + rest_block, lambda *g: (0,) + rest_index(*g)), layer, n_layers
    return pl.BlockSpec((1,) + rest_block, lambda *g: (layer,) + rest_index(*g)), 0, 1


def _zero_other_slabs(s_out_ref, slab, n_slabs):
    for other in range(n_slabs):
        if other != slab:
            s_out_ref[other] = jnp.zeros(s_out_ref.shape[1:], s_out_ref.dtype)


def _retention_kernel(*refs, has_acc, slab, n_slabs):
    p_ref, cos_ref, sin_ref, gnw_ref = refs[:4]
    o_ref, s_out_ref, s_ref = refs[4 + has_acc:]
    c = pl.program_id(1)
    L = RET_CHUNK
    heads = range(RET_HEADS)
    qk = RET_HEADS * RET_DK

    @pl.when(c == 0)
    def _():
        s_ref[...] = jnp.zeros_like(s_ref)

    dot = lambda a, b: jnp.dot(a, b, preferred_element_type=F32)
    ri = lax.broadcasted_iota(jnp.int32, (L, 1), 0).astype(F32)
    ci = lax.broadcasted_iota(jnp.int32, (1, L), 1).astype(F32)
    rel = ri - ci
    lg = [_log_g(h) for h in heads]
    cos, sin = cos_ref[...], sin_ref[...]
    q = [_rotate(p_ref[:, h * RET_DK:(h + 1) * RET_DK].astype(F32), cos, sin).astype(BF16) for h in heads]
    kf = [_rotate(p_ref[:, qk + h * RET_DK:qk + (h + 1) * RET_DK].astype(F32), cos, sin) * (RET_DK ** -0.5)
          for h in heads]
    v = [p_ref[:, 2 * qk + h * RET_DV:2 * qk + (h + 1) * RET_DV] for h in heads]
    scores = [lax.dot_general(q[h], kf[h].astype(BF16), NT_DIMS, preferred_element_type=F32)
              * jnp.where(rel >= 0, jnp.exp(lg[h] * jnp.maximum(rel, 0.0)), 0.0) for h in heads]
    s_old = [s_ref[h] for h in heads]
    qs = [dot(q[h], s_old[h].astype(BF16)) for h in heads]
    o = [dot(scores[h].astype(BF16), v[h]) + jnp.exp(lg[h] * (ri + 1.0)) * qs[h] for h in heads]
    kd = [(kf[h] * jnp.exp(lg[h] * (L - 1.0 - ri))).astype(BF16) for h in heads]
    s_new = [math.exp(lg[h] * L) * s_old[h] + lax.dot_general(kd[h], v[h], TN_DIMS, preferred_element_type=F32)
             for h in heads]
    for h in heads:
        s_ref[h] = s_new[h]
    g0 = 2 * qk + RET_VDIM
    for h in heads:
        sl = slice(h * RET_DV, (h + 1) * RET_DV)
        g = p_ref[:, g0 + h * RET_DV:g0 + (h + 1) * RET_DV].astype(F32)
        o_ref[:, sl] = _gated_head_norm(o[h], g * _sigmoid(g), gnw_ref[:, sl]).astype(o_ref.dtype)

    @pl.when(c == pl.num_programs(1) - 1)
    def _():
        s_out_ref[slab, 0] = s_ref[...]
        _zero_other_slabs(s_out_ref, slab, n_slabs)


def retention_prompt(proj, cos, sin, gnw, n_seq, chunks, layer, n_layers, acc):
    L = RET_CHUNK
    m, width = proj.shape
    ins = [proj, cos, sin, gnw]
    specs = [pl.BlockSpec((L, width), lambda b, c: (b * chunks + c, 0)),
             pl.BlockSpec((L, RET_DK // 2), lambda b, c: (c, 0)),
             pl.BlockSpec((L, RET_DK // 2), lambda b, c: (c, 0)),
             pl.BlockSpec((1, RET_VDIM), lambda b, c: (0, 0))]
    aliases = _layer_slab(acc, ins, specs, 1)
    st, slab, n_slabs = _slab_spec(acc, layer, n_layers, (1, RET_HEADS, RET_DK, RET_DV), lambda b, c: (b, 0, 0, 0))
    return pl.pallas_call(
        functools.partial(_retention_kernel, has_acc=acc is not None, slab=slab, n_slabs=n_slabs),
        grid=(n_seq, chunks),
        in_specs=specs,
        out_specs=[pl.BlockSpec((L, RET_VDIM), lambda b, c: (b * chunks + c, 0)), st],
        out_shape=[jax.ShapeDtypeStruct((m, RET_VDIM), BF16),
                   jax.ShapeDtypeStruct((n_layers, n_seq, RET_HEADS, RET_DK, RET_DV), F32)],
        scratch_shapes=[pltpu.VMEM((RET_HEADS, RET_DK, RET_DV), F32)],
        input_output_aliases=aliases,
        compiler_params=_params("parallel", "arbitrary"),
        name="retention_prompt",
    )(*ins)


def _to_column(row, n):
    eye = lax.broadcasted_iota(jnp.int32, (n, n), 0) == lax.broadcasted_iota(jnp.int32, (n, n), 1)
    return jnp.sum(jnp.where(eye, jnp.broadcast_to(row, (n, n)), 0.0), axis=-1, keepdims=True)


def _retention_step_kernel(*refs, has_acc, slab, n_slabs):
    p_ref, cos_ref, sin_ref, gnw_ref, s_ref = refs[:5]
    o_ref, s_out_ref = refs[5 + has_acc:]
    _zero_other_slabs(s_out_ref, slab, n_slabs)
    cos, sin = cos_ref[0:1, :], sin_ref[0:1, :]
    qk = RET_HEADS * RET_DK
    for h in range(RET_HEADS):
        decay = math.exp(_log_g(h))
        q = _rotate(p_ref[0, :, h * RET_DK:(h + 1) * RET_DK], cos, sin)
        k = _rotate(p_ref[0, :, qk + h * RET_DK:qk + (h + 1) * RET_DK], cos, sin) * (RET_DK ** -0.5)
        v = p_ref[0, :, 2 * qk + h * RET_DV:2 * qk + (h + 1) * RET_DV]
        g = p_ref[0, :, 2 * qk + RET_VDIM + h * RET_DV:2 * qk + RET_VDIM + (h + 1) * RET_DV]
        s_old = s_ref[0, 0, h]
        qs = jnp.sum(s_old * _to_column(q, RET_DK), axis=0, keepdims=True)
        o = jnp.sum(q * k, axis=-1, keepdims=True) * v + decay * qs
        s_out_ref[slab, 0, h] = decay * s_old + _to_column(k, RET_DK) * v
        o_ref[0, :, h * RET_DV:(h + 1) * RET_DV] = _gated_head_norm(
            o, g * _sigmoid(g), gnw_ref[:, h * RET_DV:(h + 1) * RET_DV]).astype(o_ref.dtype)


def retention_step(proj, cos, sin, gnw, states, layer, acc):
    n = proj.shape[0]
    st = pl.BlockSpec((1, 1, RET_HEADS, RET_DK, RET_DV), lambda b: (layer, b, 0, 0, 0))
    ins = [proj, cos, sin, gnw, states]
    specs = [pl.BlockSpec((1, 1, proj.shape[2]), lambda b: (b, 0, 0)),
             pl.BlockSpec(cos.shape, lambda b: (0, 0)),
             pl.BlockSpec(sin.shape, lambda b: (0, 0)),
             pl.BlockSpec((1, RET_VDIM), lambda b: (0, 0)),
             st]
    aliases = _layer_slab(acc, ins, specs, 1)
    st_out, slab, n_slabs = _slab_spec(acc, layer, states.shape[0], (1, RET_HEADS, RET_DK, RET_DV),
                                       lambda b: (b, 0, 0, 0))
    return pl.pallas_call(
        functools.partial(_retention_step_kernel, has_acc=acc is not None, slab=slab, n_slabs=n_slabs),
        grid=(n,),
        in_specs=specs,
        out_specs=[pl.BlockSpec((1, 1, RET_VDIM), lambda b: (b, 0, 0)), st_out],
        out_shape=[jax.ShapeDtypeStruct((n, 1, RET_VDIM), BF16),
                   jax.ShapeDtypeStruct(states.shape, F32)],
        input_output_aliases=aliases,
        compiler_params=_params("parallel"),
        name="retention_step",
    )(*ins)


FFN_COLS = 256


def _conv_gate(gate, p1, p2, cw, cb):
    rows = lax.broadcasted_iota(jnp.int32, (gate.shape[0], 1), 0)
    g1 = jnp.where(rows == 0, p1, pltpu.roll(gate, 1, axis=0))
    g2 = jnp.where(rows == 0, p2, jnp.where(rows == 1, p1, pltpu.roll(gate, 2, axis=0)))
    return cb + g2 * cw[0:1, :] + g1 * cw[1:2, :] + gate * cw[2:3, :]


def _ffn_up_seq_kernel(x_ref, o_ref, wo_ref, g_ref, w_ref, cw_ref, cb_ref, x1_ref, a_ref, nc_ref, carry_ref,
                       *, tiles_per_seq, lead):
    m = pl.program_id(0)

    @pl.when(m % tiles_per_seq == 0)
    def _():
        carry_ref[...] = jnp.zeros_like(carry_ref)

    x1 = x_ref[...] + jnp.dot(o_ref[...], wo_ref[...], preferred_element_type=F32)
    tm = x1.shape[0]
    row = (m % tiles_per_seq) * tm + lax.broadcasted_iota(jnp.int32, (tm, 1), 0)
    x1 = jnp.where(row >= lead, x1, 0.0)
    x1_ref[...] = x1
    h = _rms(x1, g_ref[...]).astype(BF16)
    for j in range(D_FF // FFN_COLS):
        sl = slice(j * FFN_COLS, (j + 1) * FFN_COLS)
        u = jnp.dot(h, w_ref[:, sl], preferred_element_type=F32)
        gate = jnp.dot(h, w_ref[:, D_FF + j * FFN_COLS:D_FF + (j + 1) * FFN_COLS], preferred_element_type=F32)
        conv = _conv_gate(gate, carry_ref[1:2, sl], carry_ref[0:1, sl], cw_ref[:, sl], cb_ref[:, sl])
        a_ref[:, sl] = (conv * _sigmoid(conv) * u).astype(a_ref.dtype)
        last = gate[tm - 2:tm, :]
        carry_ref[0:2, sl] = last
        nc_ref[0, :, sl] = last


def ffn_up_seq(x, o, w_o, o_layer, g, w_ug, layer, cw, cb, tm, n_seq, tiles_per_seq, lead):
    m, d = x.shape
    const = lambda a: pl.BlockSpec(a.shape, lambda i: (0, 0), pipeline_mode=pl.Buffered(1))
    stacked = lambda w, l: pl.BlockSpec((None,) + w.shape[1:], lambda i: (l, 0, 0), pipeline_mode=pl.Buffered(1))
    return pl.pallas_call(
        functools.partial(_ffn_up_seq_kernel, tiles_per_seq=tiles_per_seq, lead=lead),
        grid=(m // tm,),
        in_specs=[pl.BlockSpec((tm, d), lambda i: (i, 0)),
                  pl.BlockSpec((tm, o.shape[1]), lambda i: (i, 0)),
                  stacked(w_o, o_layer), const(g), stacked(w_ug, layer), const(cw), const(cb)],
        out_specs=[pl.BlockSpec((tm, d), lambda i: (i, 0)),
                   pl.BlockSpec((tm, D_FF), lambda i: (i, 0)),
                   pl.BlockSpec((1, 2, D_FF), lambda i: (i // tiles_per_seq, 0, 0))],
        out_shape=[jax.ShapeDtypeStruct((m, d), F32),
                   jax.ShapeDtypeStruct((m, D_FF), BF16),
                   jax.ShapeDtypeStruct((n_seq, 2, D_FF), F32)],
        scratch_shapes=[pltpu.VMEM((8, D_FF), F32)],
        compiler_params=_params("arbitrary"),
        name="ffn_up_seq",
    )(x, o, w_o, g, w_ug, cw, cb)


def _ffn_up_step_kernel(x_ref, g_ref, wu_ref, wg_ref, cw_ref, cb_ref, b0_ref, b1_ref,
                        a_ref, n0_ref, n1_ref, h_ref):
    @pl.when(pl.program_id(0) == 0)
    def _():
        h_ref[...] = _rms(x_ref[...], g_ref[...]).astype(BF16)

    h = h_ref[...]
    u = jnp.dot(h, wu_ref[...], preferred_element_type=F32)
    gate = jnp.dot(h, wg_ref[...], preferred_element_type=F32)
    b1 = b1_ref[...]
    conv = cb_ref[...] + b0_ref[...] * cw_ref[0:1, :] + b1 * cw_ref[1:2, :] + gate * cw_ref[2:3, :]
    a_ref[...] = (conv * _sigmoid(conv) * u).astype(a_ref.dtype)
    n0_ref[...] = b1
    n1_ref[...] = gate


def ffn_up_step(x, g, w_ug, layer, cw, cb, buf, tn):
    n_req, d = x.shape
    nt = D_FF // tn
    col = pl.BlockSpec((n_req, tn), lambda j: (0, j))
    return pl.pallas_call(
        _ffn_up_step_kernel,
        grid=(nt,),
        in_specs=[pl.BlockSpec((n_req, d), lambda j: (0, 0)),
                  pl.BlockSpec((1, d), lambda j: (0, 0)),
                  pl.BlockSpec((None, d, tn), lambda j: (layer, 0, j)),
                  pl.BlockSpec((None, d, tn), lambda j: (layer, 0, nt + j)),
                  pl.BlockSpec((3, tn), lambda j: (0, j)),
                  pl.BlockSpec((1, tn), lambda j: (0, j)),
                  col,
                  pl.BlockSpec((n_req, tn), lambda j: (0, nt + j))],
        out_specs=[col, col, col],
        out_shape=[jax.ShapeDtypeStruct((n_req, D_FF), BF16),
                   jax.ShapeDtypeStruct((n_req, D_FF), F32),
                   jax.ShapeDtypeStruct((n_req, D_FF), F32)],
        scratch_shapes=[pltpu.VMEM((n_req, d), BF16)],
        compiler_params=_params("arbitrary"),
        name="ffn_up_step",
    )(x, g, w_ug, w_ug, cw, cb, buf, buf)


def _pair_ones():
    r = lax.broadcasted_iota(jnp.int32, (LANES, LANES), 0) // RWKV_HEAD
    c = lax.broadcasted_iota(jnp.int32, (LANES, LANES), 1) // RWKV_HEAD
    return jnp.where(r == c, 1.0, 0.0).astype(BF16)


def _head_sum(x, ones, terms):
    total = None
    for _ in range(terms):
        piece = x.astype(BF16)
        part = jnp.dot(piece, ones, preferred_element_type=F32)
        total = part if total is None else total + part
        x = x - piece.astype(F32)
    return total


def _rwkv_proj_body(h, hprev, mu_ref, wrkv_ref, w1_ref, w2_ref, a1_ref, a2_ref, g1_ref, g2_ref, vec_ref,
                    vres, outs, mix_dtype):
    r_ref, k_ref, v_ref, kk_ref, ka_ref, ld_ref, g_ref = outs
    hm, dxm = h.astype(mix_dtype), (hprev - h).astype(mix_dtype)
    mix = lambda i: (hm + dxm * mu_ref[i:i + 1, :].astype(mix_dtype)).astype(BF16)
    xr, xw, xk, xv, xa, xg = (mix(i) for i in range(6))
    dot = lambda a, b: jnp.dot(a, b, preferred_element_type=F32)
    w0, a0, k_k, k_a = (vec_ref[i:i + 1, :] for i in range(4))
    r = dot(xr, wrkv_ref[0])
    k = dot(xk, wrkv_ref[1])
    v = dot(xv, wrkv_ref[2])
    z = w0 + dot(jnp.tanh(dot(xw, w1_ref[...])).astype(BF16), w2_ref[...])
    ld_ref[...] = -math.exp(-0.5) * _sigmoid(z)
    a = _sigmoid(a0 + dot(dot(xa, a1_ref[...]).astype(BF16), a2_ref[...]))
    g_ref[...] = dot(_sigmoid(dot(xg, g1_ref[...])).astype(BF16), g2_ref[...]).astype(g_ref.dtype)
    if vres is not None:
        vf_ref, v1_ref, v2_ref = vres
        v0 = vec_ref[4:5, :]
        v = v + (vf_ref[...] - v) * _sigmoid(v0 + dot(dot(xv, v1_ref[...]).astype(BF16), v2_ref[...]))
    kkr = k * k_k
    ones = _pair_ones()
    kk = jnp.concatenate(
        [kkr[:, t:t + LANES] * lax.rsqrt(jnp.maximum(
            _head_sum(kkr[:, t:t + LANES] * kkr[:, t:t + LANES], ones, 2), 1e-12))
         for t in range(0, D_MODEL, LANES)], axis=1)
    r_ref[...] = r.astype(r_ref.dtype)
    k_ref[...] = (k * (1.0 + (a - 1.0) * k_a)).astype(k_ref.dtype)
    v_ref[...] = v.astype(v_ref.dtype)
    kk_ref[...] = kk.astype(kk_ref.dtype)
    ka_ref[...] = (kk * a).astype(ka_ref.dtype)


def _rwkv_proj_seq_kernel(*refs, tiles_per_seq, has_vres):
    x_ref, gn_ref = refs[0], refs[1]
    weights = refs[2:11]
    n_in = 11 + (3 if has_vres else 0)
    vres = refs[11:14] if has_vres else None
    outs = refs[n_in:n_in + 7]
    shift_ref, carry_ref = refs[n_in + 7], refs[n_in + 8]
    m = pl.program_id(0)

    @pl.when(m % tiles_per_seq == 0)
    def _():
        carry_ref[...] = jnp.zeros_like(carry_ref)

    h = _rms(x_ref[...], gn_ref[...])
    tm = h.shape[0]
    rows = lax.broadcasted_iota(jnp.int32, (tm, 1), 0)
    hprev = jnp.where(rows == 0, carry_ref[0:1, :], pltpu.roll(h, 1, axis=0))
    _rwkv_proj_body(h, hprev, *weights, vres, outs, BF16)
    carry_ref[0:1, :] = h[tm - 1:tm, :]
    shift_ref[0] = h[tm - 1:tm, :]


def _rwkv_proj_step_kernel(*refs, has_vres):
    x_ref, gn_ref, prev_ref = refs[0], refs[1], refs[2]
    weights = refs[3:12]
    n_in = 12 + (3 if has_vres else 0)
    vres = refs[12:15] if has_vres else None
    outs = refs[n_in:n_in + 7]
    shift_ref = refs[n_in + 7]
    h = _rms(x_ref[...], gn_ref[...])
    _rwkv_proj_body(h, prev_ref[...], *weights, vres, outs, F32)
    shift_ref[...] = h


def rwkv_proj(x, gn, prev, wts, vres, tm, n_seq, tiles_per_seq):
    m, d = x.shape
    def full(a):
        if isinstance(a, tuple):
            w, l = a
            return pl.BlockSpec((None,) + w.shape[1:], lambda i, _n=w.ndim - 1: (l,) + (0,) * _n,
                                pipeline_mode=pl.Buffered(1))
        return pl.BlockSpec(a.shape, lambda i, _n=a.ndim: (0,) * _n, pipeline_mode=pl.Buffered(1))

    rowblk = pl.BlockSpec((tm, d), lambda i: (i, 0))
    seq = prev is None
    ins = [x, gn] + ([] if seq else [prev]) + [a[0] if isinstance(a, tuple) else a for a in wts]
    specs = [rowblk, full(gn)] + ([] if seq else [rowblk]) + [full(a) for a in wts]
    if vres is not None:
        vf, v1, v2 = vres
        ins += [vf, v1, v2]
        specs += [rowblk, full(v1), full(v2)]
    act = BF16 if seq else F32
    out_shape = [jax.ShapeDtypeStruct((m, d), dt) for dt in (act, F32, F32, F32, F32, F32, act)]
    out_specs = [rowblk] * 7
    if seq:
        out_shape.append(jax.ShapeDtypeStruct((n_seq, 1, d), F32))
        out_specs.append(pl.BlockSpec((1, 1, d), lambda i: (i // tiles_per_seq, 0, 0)))
        body = functools.partial(_rwkv_proj_seq_kernel, tiles_per_seq=tiles_per_seq, has_vres=vres is not None)
        scratch = [pltpu.VMEM((8, d), F32)]
    else:
        out_shape.append(jax.ShapeDtypeStruct((m, d), F32))
        out_specs.append(rowblk)
        body = functools.partial(_rwkv_proj_step_kernel, has_vres=vres is not None)
        scratch = []
    return pl.pallas_call(
        body,
        grid=(m // tm,),
        in_specs=specs,
        out_specs=out_specs,
        out_shape=out_shape,
        scratch_shapes=scratch,
        compiler_params=_params("arbitrary"),
        name="rwkv_proj_seq" if seq else "rwkv_proj_step",
    )(*ins)


def _wkv_epilogue(y, r, k, v, g, lnw, lnb, rk):
    inv_n = 1.0 / RWKV_HEAD
    first = lax.broadcasted_iota(jnp.int32, (1, LANES), 1) < RWKV_HEAD

    def head_sum(x):
        s0 = jnp.sum(jnp.where(first, x, 0.0), axis=-1, keepdims=True)
        s1 = jnp.sum(jnp.where(first, 0.0, x), axis=-1, keepdims=True)
        return jnp.where(first, s0, s1)

    yc = y - head_sum(y) * inv_n
    yn = yc * lax.rsqrt(head_sum(yc * yc) * inv_n + RWKV_GN_EPS)
    return (yn * lnw + lnb + head_sum(r * k * rk) * v) * g


def _stack_heads(x):
    first = lax.broadcasted_iota(jnp.int32, (1, LANES), 1) < RWKV_HEAD
    return jnp.concatenate([jnp.where(first, x, 0.0), jnp.where(first, 0.0, x)], axis=0)


def _wkv_seq_kernel(*refs, pairs, has_acc, slab, n_slabs):
    r_ref, k_ref, v_ref, kk_ref, ka_ref, ld_ref, g_ref, lnw_ref, lnb_ref, rk_ref = refs[:10]
    z_ref, s_out_ref, s_ref = refs[10 + has_acc:]
    c = pl.program_id(1)
    L = WKV_CHUNK
    R = 2 * L

    @pl.when(c == 0)
    def _():
        s_ref[...] = jnp.zeros_like(s_ref)

    dot = lambda a, b: jnp.dot(a.astype(BF16), b.astype(BF16), preferred_element_type=F32)
    dot_nt = lambda a, b: lax.dot_general(a.astype(BF16), b.astype(BF16), NT_DIMS, preferred_element_type=F32)
    dot_tn = lambda a, b: lax.dot_general(a.astype(BF16), b.astype(BF16), TN_DIMS, preferred_element_type=F32)
    ti = lax.broadcasted_iota(jnp.int32, (L, L), 0)
    tj = lax.broadcasted_iota(jnp.int32, (L, L), 1)
    tri = jnp.where(ti >= tj, 1.0, 0.0).astype(BF16)
    ri = lax.broadcasted_iota(jnp.int32, (R, R), 0)
    rj = lax.broadcasted_iota(jnp.int32, (R, R), 1)
    same = (ri // L) == (rj // L)
    lower = same & ((ri % L) > (rj % L))
    lower_eq = same & ((ri % L) >= (rj % L))
    eye = jnp.where(ri == rj, 1.0, 0.0)

    for first in range(0, pairs, WKV_GROUP):
        _wkv_group(range(first, first + WKV_GROUP), refs[:10], z_ref, s_out_ref, s_ref, c,
                   (dot, dot_nt, dot_tn, tri, lower, lower_eq, eye))

    @pl.when(c == pl.num_programs(1) - 1)
    def _():
        for p in range(pairs):
            s_out_ref[slab, 0, 2 * p] = s_ref[p, 0:RWKV_HEAD, 0:RWKV_HEAD]
            s_out_ref[slab, 0, 2 * p + 1] = s_ref[p, RWKV_HEAD:LANES, RWKV_HEAD:LANES]
        _zero_other_slabs(s_out_ref, slab, n_slabs)


def _wkv_group(P, ins, z_ref, s_out_ref, s_ref, c, consts):
    r_ref, k_ref, v_ref, kk_ref, ka_ref, ld_ref, g_ref, lnw_ref, lnb_ref, rk_ref = ins
    dot, dot_nt, dot_tn, tri, lower, lower_eq, eye = consts
    L = WKV_CHUNK
    R = 2 * L
    P = list(P)
    sls = {p: slice(p * LANES, (p + 1) * LANES) for p in P}
    ld = {p: ld_ref[:, sls[p]] for p in P}
    ld_hi = {p: ld[p].astype(BF16) for p in P}
    ld_mid = {p: (ld[p] - ld_hi[p].astype(F32)).astype(BF16) for p in P}
    ld_lo = {p: (ld[p] - ld_hi[p].astype(F32) - ld_mid[p].astype(F32)).astype(BF16) for p in P}
    cum = {p: dot(tri, ld_hi[p]) + dot(tri, ld_mid[p]) + dot(tri, ld_lo[p]) for p in P}
    tot = {p: cum[p][L - 1:L, :] for p in P}
    ar, bk, bk_end, v_s = {}, {}, {}, {}
    for p in P:
        r, k, v, kk, ka = (ref[:, sls[p]] for ref in (r_ref, k_ref, v_ref, kk_ref, ka_ref))
        dec_out = jnp.exp(-cum[p])
        dec_end = jnp.exp(tot[p] - cum[p])
        ar[p] = jnp.concatenate([_stack_heads(jnp.exp(cum[p] - ld[p]) * (-kk)),
                                 _stack_heads(jnp.exp(cum[p]) * r)], axis=0).astype(BF16)
        bk[p] = jnp.concatenate([_stack_heads(dec_out * ka), _stack_heads(dec_out * k)], axis=0).astype(BF16)
        bk_end[p] = jnp.concatenate([_stack_heads(dec_end * ka), _stack_heads(dec_end * k)], axis=0).astype(BF16)
        v_s[p] = _stack_heads(v)
    big = {p: dot_nt(ar[p], bk[p]) for p in P}
    n_ab = {p: jnp.where(lower, big[p][0:R, 0:R], 0.0) for p in P}
    a_ak = {p: jnp.where(lower, big[p][0:R, R:2 * R], 0.0).astype(BF16) for p in P}
    a_r = {p: jnp.concatenate([jnp.where(lower_eq, big[p][R:2 * R, 0:R], 0.0),
                               jnp.where(lower_eq, big[p][R:2 * R, R:2 * R], 0.0)], axis=1).astype(BF16)
           for p in P}
    inv = {p: eye + n_ab[p] for p in P}
    pw = {p: dot(n_ab[p], n_ab[p]) for p in P}
    for _ in range(int(math.log2(L)) - 2):
        both = {p: dot(jnp.concatenate([pw[p], inv[p]], axis=0), pw[p]) for p in P}
        inv = {p: inv[p] + both[p][R:2 * R] for p in P}
        pw = {p: both[p][0:R] for p in P}
    inv = {p: inv[p] + dot(inv[p], pw[p]) for p in P}
    ars = {p: dot_nt(ar[p], s_ref[p]) for p in P}
    akv = {p: dot(a_ak[p], v_s[p]) for p in P}
    u_s = {p: dot(inv[p], ars[p][0:R] + akv[p]) for p in P}
    uv = {p: jnp.concatenate([u_s[p], v_s[p]], axis=0).astype(BF16) for p in P}
    y_s = {p: ars[p][R:2 * R] + dot(a_r[p], uv[p]) for p in P}
    for p in P:
        s_ref[p] = s_ref[p] * jnp.exp(tot[p]) + dot_tn(uv[p], bk_end[p])
    for p in P:
        sl = sls[p]
        y = y_s[p][0:L, :] + y_s[p][L:R, :]
        z_ref[:, sl] = _wkv_epilogue(y, r_ref[:, sl], k_ref[:, sl], v_ref[:, sl], g_ref[:, sl], lnw_ref[:, sl],
                                     lnb_ref[:, sl], rk_ref[:, sl]).astype(z_ref.dtype)


def wkv_seq(r, k, v, kk, ka, ld, g, lnw, lnb, rk, n_seq, chunks, layer, n_layers, acc):
    m, d = r.shape
    L = WKV_CHUNK
    blk = pl.BlockSpec((L, d), lambda b, c: (b * chunks + c, 0))
    vec = pl.BlockSpec((1, d), lambda b, c: (0, 0))
    ins = [r, k, v, kk, ka, ld, g, lnw, lnb, rk]
    specs = [blk] * 7 + [vec] * 3
    aliases = _layer_slab(acc, ins, specs, 1)
    st, slab, n_slabs = _slab_spec(acc, layer, n_layers, (1, RWKV_HEADS, RWKV_HEAD, RWKV_HEAD),
                                   lambda b, c: (b, 0, 0, 0))
    return pl.pallas_call(
        functools.partial(_wkv_seq_kernel, pairs=d // LANES, has_acc=acc is not None, slab=slab, n_slabs=n_slabs),
        grid=(n_seq, chunks),
        in_specs=specs,
        out_specs=[blk, st],
        out_shape=[jax.ShapeDtypeStruct((m, d), BF16),
                   jax.ShapeDtypeStruct((n_layers, n_seq, RWKV_HEADS, RWKV_HEAD, RWKV_HEAD), F32)],
        scratch_shapes=[pltpu.VMEM((d // LANES, LANES, LANES), F32)],
        input_output_aliases=aliases,
        compiler_params=_params("parallel", "arbitrary"),
        name="wkv_seq",
    )(*ins)


def _wkv_step_kernel(*refs, has_acc, slab, n_slabs):
    r_ref, k_ref, v_ref, kk_ref, ka_ref, ld_ref, g_ref, lnw_ref, lnb_ref, rk_ref, s_ref = refs[:11]
    z_ref, s_out_ref, vt_ref, y_ref = refs[11 + has_acc:]
    _zero_other_slabs(s_out_ref, slab, n_slabs)
    N = RWKV_HEAD
    SUB = 8
    r, k, v, nkk, ka, g = (ref[...].T for ref in (r_ref, k_ref, v_ref, kk_ref, ka_ref, g_ref))
    nkk = -nkk
    w = jnp.exp(ld_ref[...].T)
    vt_ref[...] = v
    sub = lax.broadcasted_iota(jnp.int32, (SUB, 1), 0)
    for hh in range(2):
        rows = slice(hh * N, (hh + 1) * N)
        r_h, k_h, nkk_h, ka_h, w_h = r[rows], k[rows], nkk[rows], ka[rows], w[rows]

        def value_rows(blk, carry, hh=hh, r_h=r_h, k_h=k_h, nkk_h=nkk_h, ka_h=ka_h, w_h=w_h):
            base = pl.multiple_of(hh * N + blk * SUB, SUB)
            v_blk = vt_ref[pl.ds(base, SUB), :]
            y_blk = jnp.zeros((SUB, v_blk.shape[1]), F32)
            for ii in range(SUB):
                s_old = s_ref[0, hh, blk * SUB + ii]
                sa = jnp.sum(s_old * nkk_h, axis=0, keepdims=True)
                s_new = s_old * w_h + sa * ka_h + v_blk[ii:ii + 1, :] * k_h
                s_out_ref[slab, hh, blk * SUB + ii] = s_new
                y_blk = jnp.where(sub == ii, jnp.sum(s_new * r_h, axis=0, keepdims=True), y_blk)
            y_ref[pl.ds(base, SUB), :] = y_blk
            return carry

        lax.fori_loop(0, N // SUB, value_rows, 0)

    def per_head(x, op):
        return jnp.concatenate([jnp.broadcast_to(op(x[hh * N:(hh + 1) * N], axis=0, keepdims=True), (N, x.shape[1]))
                                for hh in range(2)], axis=0)

    y = y_ref[...]
    yc = y - per_head(y, jnp.mean)
    yn = yc * lax.rsqrt(per_head(yc * yc, jnp.mean) + RWKV_GN_EPS)
    bonus = per_head(r * k * rk_ref[...], jnp.sum) * v
    z_ref[...] = ((yn * lnw_ref[...] + lnb_ref[...] + bonus) * g).T


def wkv_step(r, k, v, kk, ka, ld, g, lnw, lnb, rk, states_t, layer, acc):
    n, d = r.shape
    tile = pl.BlockSpec((n, LANES), lambda p: (0, p))
    col = pl.BlockSpec((LANES, 1), lambda p: (p, 0))
    st = pl.BlockSpec((1, 2) + states_t.shape[2:], lambda p: (layer, p, 0, 0, 0))
    ins = [r, k, v, kk, ka, ld, g, lnw, lnb, rk, states_t]
    specs = [tile] * 7 + [col] * 3 + [st]
    aliases = _layer_slab(acc, ins, specs, 1)
    st_out, slab, n_slabs = _slab_spec(acc, layer, states_t.shape[0], (2,) + states_t.shape[2:],
                                       lambda p: (p, 0, 0, 0))
    return pl.pallas_call(
        functools.partial(_wkv_step_kernel, has_acc=acc is not None, slab=slab, n_slabs=n_slabs),
        grid=(d // LANES,),
        in_specs=specs,
        out_specs=[tile, st_out],
        out_shape=[jax.ShapeDtypeStruct((n, d), F32), jax.ShapeDtypeStruct(states_t.shape, F32)],
        scratch_shapes=[pltpu.VMEM((LANES, n), F32), pltpu.VMEM((LANES, n), F32)],
        input_output_aliases=aliases,
        compiler_params=_params("parallel"),
        name="wkv_step",
    )(*ins)


def _trunk(x, seq, states, p):
    depth = p['norm_mix'].shape[0]
    rows = x.shape[0]
    if seq is not None:
        n_seq, tp, lead = seq
        tm = _row_tile(tp, 1088)
        tm_small = _row_tile(tp, 544)
        tiles, tiles_small = tp // tm, tp // tm_small
        cos, sin = rot_table(tp, -lead, 1)
    else:
        ret_s, wkv_s, shift_s, conv_s = states
        wkv_s = jnp.transpose(wkv_s, (0, 2, 3, 4, 1))
        tm = tm_small = rows
        tiles = tiles_small = 1
        lead = 0
        cos, sin = rot_table(8, PAST_LEN, 0)
    new_ret = new_wkv = None
    new_shift, new_conv = [], []
    n_ret, n_rwkv = (depth + 1) // 2, depth // 2
    v_first = None
    for i in range(depth):
        j = i // 2
        gn = p['norm_mix'][i][None]
        if i % 2 == 0:
            gnw = p['ret_gn_w'][j][None]
            if seq is not None:
                proj = norm_proj(x, gn, p['ret_w_in'], j, tm, 3072, BF16)
                o, new_ret = retention_prompt(proj, cos, sin, gnw, n_seq, tp // RET_CHUNK, j, n_ret, new_ret)
            else:
                proj = norm_proj(x, gn, p['ret_w_in'], j, tm, 1024, F32)
                o, new_ret = retention_step(proj[:, None, :], cos, sin, gnw, ret_s, j, new_ret)
                o = o[:, 0, :]
            mix_out, w_mix = o, p['ret_w_out']
        else:
            vecs = [p['rwkv_w0'][j], p['rwkv_a0'][j], p['rwkv_k_k'][j], p['rwkv_k_a'][j]]
            vecs.append(p['rwkv_v0'][j - 1] if j else jnp.zeros_like(vecs[0]))
            vecs = jnp.stack(vecs + [jnp.zeros_like(vecs[0])] * 3)
            wts = [p['rwkv_mu'][j], (p['rwkv_w_rkv'], j), p['rwkv_w1'][j], p['rwkv_w2'][j], p['rwkv_a1'][j],
                   p['rwkv_a2'][j], p['rwkv_g1'][j], p['rwkv_g2'][j], vecs]
            vres = (v_first, p['rwkv_v1'][j - 1], p['rwkv_v2'][j - 1]) if j else None
            lnw, lnb = p['rwkv_ln_w'][j][None], p['rwkv_ln_b'][j][None]
            rk = p['rwkv_r_k'][j].reshape(1, D_MODEL)
            if seq is not None:
                r, k, v, kk, ka, ld, g, sh = rwkv_proj(x, gn, None, wts, vres, tm_small, n_seq, tiles_small)
                z, new_wkv = wkv_seq(r, k, v, kk, ka, ld, g, lnw, lnb, rk, n_seq, tp // WKV_CHUNK,
                                     j, n_rwkv, new_wkv)
                sh = sh[:, 0, :]
            else:
                r, k, v, kk, ka, ld, g, sh = rwkv_proj(x, gn, shift_s[j], wts, vres, tm, 1, 1)
                z, new_wkv = wkv_step(r, k, v, kk, ka, ld, g, *(t.reshape(D_MODEL, 1) for t in (lnw, lnb, rk)),
                                      wkv_s, j, new_wkv)
            if v_first is None:
                v_first = v
            new_shift.append(sh)
            mix_out, w_mix = z, p['rwkv_w_o']
        gf = p['norm_ffn'][i][None]
        cw, cb = p['ffn_conv_w'][i], p['ffn_conv_b'][i][None]
        if seq is not None:
            x, a, cbuf = ffn_up_seq(x, mix_out, w_mix, j, gf, p['ffn_w_ug'], i, cw, cb, tm_small, n_seq,
                                    tiles_small, lead)
        else:
            x = matmul_res(mix_out, w_mix, j, x, tm_small, tiles_small, lead)
            a, n0, n1 = ffn_up_step(x, gf, p['ffn_w_ug'], i, cw, cb, conv_s[i].reshape(rows, 2 * D_FF), D_FF // 2)
            cbuf = jnp.stack([n0, n1], axis=1)
        new_conv.append(cbuf)
        x = matmul_res(a, p['ffn_w_d'], i, x, tm, tiles, lead)
    if seq is None:
        new_wkv = jnp.transpose(new_wkv, (0, 4, 1, 2, 3))
    return x, new_ret, new_wkv, jnp.stack(new_shift), jnp.stack(new_conv)


def kernel(x_prompt, x_sample, state_ret, state_wkv, state_shift, state_conv, meta_tokens, norm_mix, norm_ffn, norm_final, ret_w_in, ret_gn_w, ret_w_out, rwkv_mu, rwkv_w_rkv, rwkv_w0, rwkv_w1, rwkv_w2, rwkv_a0, rwkv_a1, rwkv_a2, rwkv_v0, rwkv_v1, rwkv_v2, rwkv_g1, rwkv_g2, rwkv_k_k, rwkv_k_a, rwkv_r_k, rwkv_ln_w, rwkv_ln_b, rwkv_w_o, ffn_w_ug, ffn_conv_w, ffn_conv_b, ffn_w_d):
    bf = lambda w: [w[i].astype(BF16) for i in range(w.shape[0])]
    bfs = lambda w: w.astype(BF16)
    p = dict(norm_mix=norm_mix, norm_ffn=norm_ffn, ret_w_in=bfs(ret_w_in), ret_gn_w=ret_gn_w,
             ret_w_out=bfs(ret_w_out), rwkv_mu=rwkv_mu, rwkv_w_rkv=bfs(rwkv_w_rkv), rwkv_w0=rwkv_w0,
             rwkv_w1=bf(rwkv_w1), rwkv_w2=bf(rwkv_w2), rwkv_a0=rwkv_a0, rwkv_a1=bf(rwkv_a1),
             rwkv_a2=bf(rwkv_a2), rwkv_v0=rwkv_v0, rwkv_v1=bf(rwkv_v1), rwkv_v2=bf(rwkv_v2),
             rwkv_g1=bf(rwkv_g1), rwkv_g2=bf(rwkv_g2), rwkv_k_k=rwkv_k_k, rwkv_k_a=rwkv_k_a,
             rwkv_r_k=rwkv_r_k, rwkv_ln_w=rwkv_ln_w, rwkv_ln_b=rwkv_ln_b, rwkv_w_o=bfs(rwkv_w_o),
             ffn_w_ug=bfs(ffn_w_ug), ffn_conv_w=ffn_conv_w, ffn_conv_b=ffn_conv_b, ffn_w_d=bfs(ffn_w_d))
    B, S, D = x_prompt.shape
    lead = (-N_META) % RET_CHUNK
    tp = lead + N_META + S
    assert tp % RET_CHUNK == 0 and D == D_MODEL
    meta = jnp.broadcast_to(meta_tokens[None].astype(F32), (B, N_META, D))
    xp = jnp.concatenate([jnp.zeros((B, lead, D), F32), meta, x_prompt.astype(F32)], axis=1).reshape(B * tp, D)
    xp, p_ret, p_wkv, p_shift, p_conv = _trunk(xp, (B, tp, lead), None, p)
    y_prompt = final_norm(xp.reshape(B, tp, D), norm_final[None], lead + N_META)

    n_req = x_sample.shape[0]
    xs, s_ret, s_wkv, s_shift, s_conv = _trunk(
        x_sample.reshape(n_req, D).astype(F32), None, (state_ret, state_wkv, state_shift, state_conv), p)
    y_sample = final_norm(xs[None], norm_final[None], 0).reshape(n_req, 1, D)
    return (y_prompt.astype(x_prompt.dtype), y_sample.astype(x_sample.dtype),
            p_ret, p_wkv, p_shift, p_conv, s_ret, s_wkv, s_shift, s_conv)
```
